```python
import jax, jax.numpy as jnp
from jax import lax
import numpy as np

D_MODEL = 2048
BATCH = 4
SEQ = 2048
DEPTH = 2
DEC_BATCH = 128
DEC_SEQ = 4
PAST_LEN = 16384
PAGE_SIZE = 128

N_META = 16
D_CONV = D_MODEL
CONV_W = 3
GLA_HEADS = 4
GLA_DK = D_MODEL // 2 // GLA_HEADS
GLA_DV = D_MODEL // GLA_HEADS
GLA_HK = GLA_HEADS * GLA_DK
GLA_HV = GLA_HEADS * GLA_DV
GATE_RANK = 16
GATE_NORM = 16.0
CHUNK = 64
D_FF = 4 * D_MODEL
ALPHA = (2 * DEPTH) ** 0.25
BETA = (8 * DEPTH) ** -0.25
EPS = 1e-5
SPLITS = (D_CONV, D_CONV, D_CONV, GLA_HK, GLA_HK, GLA_HV, GLA_HV, GATE_RANK, D_MODEL, D_MODEL)
D_IN_PROJ = D_CONV * 3 + GLA_HK * 2 + GLA_HV * 2 + GATE_RANK + D_MODEL * 2

kernel_name = 'gated_conv_gla_hybrid_step'


def layer_norm(x, w, b):
    xf = x.astype(jnp.float32)
    mu = jnp.mean(xf, axis=-1, keepdims=True)
    var = jnp.mean(jnp.square(xf - mu), axis=-1, keepdims=True)
    y = (xf - mu) * lax.rsqrt(var + EPS)
    return (y * w.astype(jnp.float32) + b.astype(jnp.float32)).astype(x.dtype)


def short_conv(u, state, w):
    L = u.shape[1]
    full = jnp.concatenate([state.astype(u.dtype), u], axis=1)
    y = full[:, 0:L] * w[0]
    for j in range(1, CONV_W):
        y = y + full[:, j:j + L] * w[j]
    return y, full[:, L:]


def gla_chunk(s, inp):
    q, k, v, gk = inp
    c = q.shape[2]
    b = jnp.cumsum(gk, axis=2)
    causal = jnp.tril(jnp.ones((c, c), dtype=bool))
    diff = b[:, :, :, None, :] - b[:, :, None, :, :]
    decay = jnp.exp(jnp.where(causal[None, None, :, :, None], diff, -jnp.inf))
    a = jnp.einsum('nhtd,nhsd,nhtsd->nhts', q, k, decay)
    o = jnp.einsum('nhts,nhsv->nhtv', a, v) + jnp.einsum('nhtd,nhdv->nhtv', q * jnp.exp(b), s)
    b_last = b[:, :, -1:, :]
    s_new = jnp.exp(b_last[:, :, 0, :])[..., None] * s + jnp.einsum('nhsd,nhsv->nhdv', k * jnp.exp(b_last - b), v)
    return s_new, o


def gla_sequence(q, k, v, gk, s0, n_lead):
    s = s0.astype(jnp.float32)
    outs = []
    L = q.shape[2]
    lead = n_lead + (L - n_lead) % CHUNK
    bounds = [0, n_lead, lead] if n_lead else [0, lead]
    for lo, hi in zip(bounds[:-1], bounds[1:]):
        if hi > lo:
            s, o = gla_chunk(s, (q[:, :, lo:hi], k[:, :, lo:hi], v[:, :, lo:hi], gk[:, :, lo:hi]))
            outs.append(o)
    n_chunks = (L - lead) // CHUNK
    if n_chunks:
        N, H = q.shape[0], q.shape[1]
        def to_chunks(a):
            return jnp.moveaxis(a[:, :, lead:].reshape(N, H, n_chunks, CHUNK, a.shape[-1]), 2, 0)
        s, o = lax.scan(gla_chunk, s, (to_chunks(q), to_chunks(k), to_chunks(v), to_chunks(gk)))
        outs.append(jnp.moveaxis(o, 0, 2).reshape(N, H, n_chunks * CHUNK, o.shape[-1]))
    return jnp.concatenate(outs, axis=2), s


def token_mixer(u, conv_state, gla_state, n_lead, w_in, conv_w, w_a2, b_a2, gn_w, w_o):
    N, L, _ = u.shape
    idx, acc = [], 0
    for sz in SPLITS[:-1]:
        acc += sz
        idx.append(acc)
    proj = u @ w_in
    cb, cc, cx, q, k, v, g, a1, ga, gb = jnp.split(proj, idx, axis=-1)
    conv_out, conv_new = short_conv(cc * cx, conv_state, conv_w)
    y_a = cb * conv_out
    gk = jax.nn.log_sigmoid((a1 @ w_a2 + b_a2).astype(jnp.float32)) / GATE_NORM
    def heads(a, d):
        return a.reshape(N, L, GLA_HEADS, d).transpose(0, 2, 1, 3).astype(jnp.float32)
    o, gla_new = gla_sequence(heads(q, GLA_DK) * (GLA_DK ** -0.5), heads(k, GLA_DK),
                              heads(v, GLA_DV), heads(gk, GLA_DK), gla_state, n_lead)
    o = o * lax.rsqrt(jnp.mean(jnp.square(o), axis=-1, keepdims=True) + EPS) * gn_w.astype(jnp.float32)
    o = o.transpose(0, 2, 1, 3).reshape(N, L, GLA_HV).astype(u.dtype)
    y_b = o * jax.nn.silu(g)
    m = jax.nn.sigmoid(ga) * y_a + jax.nn.sigmoid(gb) * y_b
    return m @ w_o, conv_new, gla_new


def trunk(h, conv_states, gla_states, n_lead, ln_in_w, ln_in_b, w_in, conv_w, w_a2, b_a2, gn_w,
          w_o, ln1_w, ln1_b, w1, w2, ln2_w, ln2_b):
    h = layer_norm(h, ln_in_w, ln_in_b)
    conv_new, gla_new = [], []
    for l in range(DEPTH):
        mix, cs, gs = token_mixer(h, conv_states[l], gla_states[l], n_lead, w_in[l], conv_w[l],
                                  w_a2[l], b_a2[l], gn_w[l], w_o[l])
        h = layer_norm(ALPHA * h + mix, ln1_w[l], ln1_b[l])
        f = jnp.square(jax.nn.relu(h @ w1[l])) @ w2[l]
        h = layer_norm(ALPHA * h + f, ln2_w[l], ln2_b[l])
        conv_new.append(cs)
        gla_new.append(gs)
    return h, jnp.stack(conv_new), jnp.stack(gla_new)


def setup_inputs(seed: int = 0) -> dict:
    key = jax.random.key(seed)
    ks = jax.random.split(key, 20)
    f32 = jnp.float32
    n = lambda k, shape, s: jax.random.normal(k, shape, f32) * s
    return {
        'x_prompt': n(ks[0], (BATCH, SEQ, D_MODEL), 1.0),
        'x_sample': n(ks[1], (DEC_BATCH, DEC_SEQ, D_MODEL), 1.0),
        'state_conv': n(ks[2], (DEPTH, DEC_BATCH, CONV_W - 1, D_CONV), 1.0),
        'state_gla': n(ks[3], (DEPTH, DEC_BATCH, GLA_HEADS, GLA_DK, GLA_DV), 1.0),
        'meta': n(ks[4], (N_META, D_MODEL), 1.0),
        'ln_in_w': 1.0 + n(ks[5], (D_MODEL,), 0.02),
        'ln_in_b': n(ks[6], (D_MODEL,), 0.02),
        'w_in': n(ks[7], (DEPTH, D_MODEL, D_IN_PROJ), D_MODEL ** -0.5),
        'conv_w': n(ks[8], (DEPTH, CONV_W, D_CONV), CONV_W ** -0.5),
        'w_a2': n(ks[9], (DEPTH, GATE_RANK, GLA_HK), GATE_RANK ** -0.5),
        'b_a2': n(ks[10], (DEPTH, GLA_HK), 0.1),
        'gn_w': 1.0 + n(ks[11], (DEPTH, GLA_DV), 0.02),
        'w_o': n(ks[12], (DEPTH, D_MODEL, D_MODEL), BETA * D_MODEL ** -0.5),
        'ln1_w': 1.0 + n(ks[13], (DEPTH, D_MODEL), 0.02),
        'ln1_b': n(ks[14], (DEPTH, D_MODEL), 0.02),
        'w1': n(ks[15], (DEPTH, D_MODEL, D_FF), D_MODEL ** -0.5),
        'w2': n(ks[16], (DEPTH, D_FF, D_MODEL), BETA * D_FF ** -0.5),
        'ln2_w': 1.0 + n(ks[17], (DEPTH, D_MODEL), 0.02),
        'ln2_b': n(ks[18], (DEPTH, D_MODEL), 0.02),
    }


def reference(x_prompt, x_sample, state_conv, state_gla, meta, ln_in_w, ln_in_b, w_in, conv_w,
              w_a2, b_a2, gn_w, w_o, ln1_w, ln1_b, w1, w2, ln2_w, ln2_b):
    params = (ln_in_w, ln_in_b, w_in, conv_w, w_a2, b_a2, gn_w, w_o, ln1_w, ln1_b, w1, w2, ln2_w, ln2_b)
    nb = x_prompt.shape[0]
    h = jnp.concatenate([jnp.broadcast_to(meta.astype(x_prompt.dtype), (nb, N_META, D_MODEL)), x_prompt], axis=1)
    conv0 = jnp.zeros((DEPTH, nb, CONV_W - 1, D_CONV), x_prompt.dtype)
    gla0 = jnp.zeros((DEPTH, nb, GLA_HEADS, GLA_DK, GLA_DV), jnp.float32)
    hp, conv_p, gla_p = trunk(h, conv0, gla0, N_META, *params)
    y_prompt = hp[:, N_META:]
    y_sample, conv_s, gla_s = trunk(x_sample, state_conv, state_gla, 0, *params)
    return (y_prompt, y_sample, conv_p, gla_p, conv_s, gla_s)
```

```python
import functools

import jax
import jax.numpy as jnp
from jax import lax
from jax.experimental import pallas as pl
from jax.experimental.pallas import tpu as pltpu

GLA_HEADS = 4
GATE_NORM = 16.0
CONV_W = 3
CHUNK = 64
SUB = 8
EPS = 1e-5
LANES = 128
VMEM_LIMIT = 56 * 1024 * 1024

F32 = jnp.float32
BF16 = jnp.bfloat16


def _params(*sem):
    return pltpu.CompilerParams(dimension_semantics=sem, vmem_limit_bytes=VMEM_LIMIT)


def _pick_tile(n, cap):
    best = None
    for t in range(8, min(n, cap) + 1, 8):
        if n % t == 0:
            best = t
    assert best is not None, (n, cap)
    return best


def _layer_norm(x, w, b):
    mu = jnp.mean(x, axis=-1, keepdims=True)
    xc = x - mu
    var = jnp.mean(xc * xc, axis=-1, keepdims=True)
    return xc * lax.rsqrt(var + EPS) * w + b


def _split3(x):
    hi = x.astype(BF16)
    r1 = x - hi.astype(F32)
    mid = r1.astype(BF16)
    lo = (r1 - mid.astype(F32)).astype(BF16)
    return hi, mid, lo


def _ln_kernel(x_ref, w_ref, b_ref, o_ref, ob_ref):
    y = _layer_norm(x_ref[...], w_ref[...], b_ref[...])
    o_ref[...] = y
    ob_ref[...] = y.astype(BF16)


def _ln(x, w, b):
    r, d = x.shape
    tm = _pick_tile(r, 1024)
    return pl.pallas_call(
        _ln_kernel,
        grid=(r // tm,),
        in_specs=[pl.BlockSpec((tm, d), lambda i: (i, 0)),
                  pl.BlockSpec((1, d), lambda i: (0, 0)),
                  pl.BlockSpec((1, d), lambda i: (0, 0))],
        out_specs=[pl.BlockSpec((tm, d), lambda i: (i, 0)),
                   pl.BlockSpec((tm, d), lambda i: (i, 0))],
        out_shape=[jax.ShapeDtypeStruct((r, d), F32), jax.ShapeDtypeStruct((r, d), BF16)],
        compiler_params=_params("parallel"),
        name="ln_in",
    )(x, w.reshape(1, d), b.reshape(1, d))


def _gate_kernel(hb_ref, wa1_ref, wa2_ref, ba2_ref, gk_ref):
    a1 = jnp.dot(hb_ref[...], wa1_ref[...], preferred_element_type=F32)
    z = jnp.dot(a1.astype(BF16), wa2_ref[...], preferred_element_type=F32) + ba2_ref[...]
    ls = jnp.minimum(z, 0.0) - jnp.log1p(jnp.exp(-jnp.abs(z)))
    gk_ref[...] = ls / GATE_NORM


def _gate(hb, wa1, wa2, ba2):
    r, d = hb.shape
    hk = wa2.shape[1]
    tm = _pick_tile(r, 1024)
    return pl.pallas_call(
        _gate_kernel,
        grid=(r // tm,),
        in_specs=[pl.BlockSpec((tm, d), lambda i: (i, 0)),
                  pl.BlockSpec(wa1.shape, lambda i: (0, 0)),
                  pl.BlockSpec(wa2.shape, lambda i: (0, 0)),
                  pl.BlockSpec((1, hk), lambda i: (0, 0))],
        out_specs=pl.BlockSpec((tm, hk), lambda i: (i, 0)),
        out_shape=jax.ShapeDtypeStruct((r, hk), F32),
        compiler_params=_params("parallel"),
        name="gate",
    )(hb, wa1, wa2, ba2.reshape(1, hk))


def _mm_kernel(x_ref, w_ref, o_ref, *, relu2):
    acc = jnp.dot(x_ref[...], w_ref[...], preferred_element_type=F32)
    if relu2:
        acc = jnp.square(jnp.maximum(acc, 0.0))
    o_ref[...] = acc.astype(o_ref.dtype)


def _mm(x, w, relu2):
    r, k = x.shape
    n = w.shape[1]
    tm = _pick_tile(r, 1032)
    tn = 1024 if n % 1024 == 0 else n
    return pl.pallas_call(
        functools.partial(_mm_kernel, relu2=relu2),
        grid=(n // tn, r // tm),
        in_specs=[pl.BlockSpec((tm, k), lambda j, i: (i, 0)),
                  pl.BlockSpec((k, tn), lambda j, i: (0, j))],
        out_specs=pl.BlockSpec((tm, tn), lambda j, i: (i, j)),
        out_shape=jax.ShapeDtypeStruct((r, n), BF16),
        compiler_params=_params("parallel", "parallel"),
        name="mm_relu2" if relu2 else "mm_proj",
    )(x, w)


def _mm_res_ln_kernel(a_ref, w_ref, res_ref, lw_ref, lb_ref, o_ref, ob_ref, *, alpha, nk):
    kk = pl.program_id(1)
    part = jnp.dot(a_ref[...], w_ref[...], preferred_element_type=F32)

    @pl.when(kk == 0)
    def _():
        o_ref[...] = part

    @pl.when(kk > 0)
    def _():
        o_ref[...] += part

    @pl.when(kk == nk - 1)
    def _():
        y = _layer_norm(alpha * res_ref[...] + o_ref[...], lw_ref[...], lb_ref[...])
        o_ref[...] = y
        ob_ref[...] = y.astype(BF16)


def _mm_res_ln(a, w, res, lw, lb, alpha):
    r, k = a.shape
    d = w.shape[1]
    tm = _pick_tile(r, 688)
    tk = 1024 if k % 1024 == 0 else k
    nk = k // tk
    return pl.pallas_call(
        functools.partial(_mm_res_ln_kernel, alpha=alpha, nk=nk),
        grid=(r // tm, nk),
        in_specs=[pl.BlockSpec((tm, tk), lambda i, kk: (i, kk)),
                  pl.BlockSpec((tk, d), lambda i, kk: (kk, 0)),
                  pl.BlockSpec((tm, d), lambda i, kk: (i, 0)),
                  pl.BlockSpec((1, d), lambda i, kk: (0, 0)),
                  pl.BlockSpec((1, d), lambda i, kk: (0, 0))],
        out_specs=[pl.BlockSpec((tm, d), lambda i, kk: (i, 0)),
                   pl.BlockSpec((tm, d), lambda i, kk: (i, 0))],
        out_shape=[jax.ShapeDtypeStruct((r, d), F32), jax.ShapeDtypeStruct((r, d), BF16)],
        compiler_params=_params("parallel", "arbitrary"),
        name="mm_res_ln",
    )(a, w, res, lw.reshape(1, d), lb.reshape(1, d))


def _cum_matrices(c, group):
    t = lax.broadcasted_iota(jnp.int32, (c, c), 0)
    u = lax.broadcasted_iota(jnp.int32, (c, c), 1)
    same = (t // group) == (u // group)
    tri = same & (u <= t)
    if group == c:
        tend = (u // SUB) <= (t // SUB)
    else:
        tend = same
    m = jnp.concatenate([tri, tend], axis=0).astype(BF16)
    return jnp.concatenate([m, m, m], axis=1)


def _cumsum2(cm, gk):
    c = gk.shape[0]
    hi, mid, lo = _split3(gk)
    g3 = jnp.concatenate([hi, mid, lo], axis=0)
    be = jnp.dot(cm, g3, preferred_element_type=F32)
    return be[:c], be[c:]


def _diag_lags(qf, kf, b, nlag):
    cols = []
    for d in range(nlag):
        if d == 0:
            p = qf * kf
        else:
            kr = pltpu.roll(kf, d, axis=0)
            br = pltpu.roll(b, d, axis=0)
            p = qf * kr * jnp.exp(jnp.minimum(b - br, 0.0))
        cols.append(jnp.sum(p, axis=-1, keepdims=True))
    return cols


def _decay_columns(b_last_row):
    dk = b_last_row.shape[1]
    hi, mid, lo = _split3(b_last_row)
    lm = jnp.concatenate([hi, mid, lo, jnp.zeros((16 - 3, dk), BF16)], axis=0)
    ones = jnp.ones((16, LANES), BF16)
    col = lax.dot_general(lm, ones, (((0,), (0,)), ((), ())), preferred_element_type=F32)
    return jnp.exp(col)


def _state_update(s_ref, edec, upd):
    dv = upd.shape[1]
    for cb in range(dv // LANES):
        sl = slice(cb * LANES, (cb + 1) * LANES)
        s_ref[:, sl] = s_ref[:, sl] * edec + upd[:, sl]


def _gla_chunk(q_ref, k_ref, v_ref, gk_ref, gnw, o_ref, s_ref, r0, c, cm, scale):
    rows = pl.ds(r0, c)
    gk = gk_ref[rows, :]
    qf = q_ref[rows, :].astype(F32) * scale
    kf = k_ref[rows, :].astype(F32)
    vb = v_ref[rows, :]
    nsub = c // SUB

    b, eb = _cumsum2(cm, gk)
    b_last = b[c - 1:c, :]

    s_bf = s_ref[...].astype(BF16)
    o = jnp.dot((qf * jnp.exp(b)).astype(BF16), s_bf, preferred_element_type=F32)

    row = lax.broadcasted_iota(jnp.int32, (c, c), 0)
    lane = lax.broadcasted_iota(jnp.int32, (c, c), 1)
    a = jnp.zeros((c, c), F32)
    for d, col in enumerate(_diag_lags(qf, kf, b, SUB)):
        a = jnp.where((lane == row - d) & ((row % SUB) >= d), col, a)

    khat = kf * jnp.exp(eb - b)
    if nsub > 1:
        pieces = []
        for j in range(nsub - 1):
            lo = SUB * (j + 1)
            pieces.append(qf[lo:] * jnp.exp(b[lo:] - eb[SUB * j:SUB * j + 1, :]))
        qs = jnp.concatenate(pieces, axis=0).astype(BF16)
        prod = lax.dot_general(qs, khat.astype(BF16), (((1,), (1,)), ((), ())),
                               preferred_element_type=F32)
        off = 0
        segs = []
        for j in range(nsub - 1):
            lo = SUB * (j + 1)
            seg = prod[off:off + c - lo, :]
            off += c - lo
            lane_j = lax.broadcasted_iota(jnp.int32, seg.shape, 1)
            seg = jnp.where((lane_j >= SUB * j) & (lane_j < SUB * (j + 1)), seg, 0.0)
            segs.append(jnp.concatenate([jnp.zeros((lo, c), F32), seg], axis=0))
        for seg in segs:
            a = a + seg

    o = o + jnp.dot(a.astype(BF16), vb, preferred_element_type=F32)
    o = o * lax.rsqrt(jnp.mean(o * o, axis=-1, keepdims=True) + EPS) * gnw
    o_ref[rows, :] = o.astype(o_ref.dtype)

    kl = (khat * jnp.exp(b_last - eb)).astype(BF16)
    upd = lax.dot_general(kl, vb, (((0,), (0,)), ((), ())), preferred_element_type=F32)
    _state_update(s_ref, _decay_columns(b_last), upd)


def _gla_seq_kernel(q_ref, k_ref, v_ref, gk_ref, gnw_ref, o_ref, s_ref, *, lead, scale):
    length = q_ref.shape[0]
    gnw = gnw_ref[...]
    s_ref[...] = jnp.zeros_like(s_ref)
    if lead:
        _gla_chunk(q_ref, k_ref, v_ref, gk_ref, gnw, o_ref, s_ref, 0, lead,
                   _cum_matrices(lead, lead), scale)
    n_chunks = (length - lead) // CHUNK
    cm = _cum_matrices(CHUNK, CHUNK)

    def body(ci, carry):
        r0 = pl.multiple_of(lead + ci * CHUNK, SUB)
        _gla_chunk(q_ref, k_ref, v_ref, gk_ref, gnw, o_ref, s_ref, r0, CHUNK, cm, scale)
        return carry

    lax.fori_loop(0, n_chunks, body, 0)


def _gla_seq(p, gk, gnw, nseq, d):
    rows = p.shape[0]
    length = rows // nseq
    dk, dv = d // 2 // GLA_HEADS, d // GLA_HEADS
    lead = (length % CHUNK)
    assert lead % SUB == 0 and length % 8 == 0
    qb, kb, vb = 3 * d // dk, 3 * d // dk + GLA_HEADS, 4 * d // dv
    return pl.pallas_call(
        functools.partial(_gla_seq_kernel, lead=lead, scale=float(dk) ** -0.5),
        grid=(nseq, GLA_HEADS),
        in_specs=[pl.BlockSpec((length, dk), lambda n, h: (n, qb + h)),
                  pl.BlockSpec((length, dk), lambda n, h: (n, kb + h)),
                  pl.BlockSpec((length, dv), lambda n, h: (n, vb + h)),
                  pl.BlockSpec((length, dk), lambda n, h: (n, h)),
                  pl.BlockSpec((1, dv), lambda n, h: (0, 0))],
        out_specs=[pl.BlockSpec((length, dv), lambda n, h: (n, h)),
                   pl.BlockSpec((None, None, dk, dv), lambda n, h: (n, h, 0, 0))],
        out_shape=[jax.ShapeDtypeStruct((rows, d), F32),
                   jax.ShapeDtypeStruct((nseq, GLA_HEADS, dk, dv), F32)],
        compiler_params=_params("parallel", "parallel"),
        name="gla_seq",
    )(p, p, p, gk, gnw.reshape(1, dv))


def _gla_step_kernel(q_ref, k_ref, v_ref, gk_ref, gnw_ref, s_in_ref, o_ref, s_ref, *, steps, scale):
    c = q_ref.shape[0]
    nb = c // steps
    per8 = 8 // steps
    gk = gk_ref[...]
    qf = q_ref[...].astype(F32) * scale
    kf = k_ref[...].astype(F32)
    vb = v_ref[...]
    vf = vb.astype(F32)

    b, eb = _cumsum2(_cum_matrices(c, steps), gk)
    row = lax.broadcasted_iota(jnp.int32, (c, 1), 0)
    o = jnp.zeros(vf.shape, F32)
    for d, col in enumerate(_diag_lags(qf, kf, b, steps)):
        vr = vf if d == 0 else pltpu.roll(vf, d, axis=0)
        o = o + jnp.where((row % steps) >= d, col, 0.0) * vr

    qe = (qf * jnp.exp(b)).astype(BF16)
    kl = kf * jnp.exp(eb - b)
    row8 = lax.broadcasted_iota(jnp.int32, (8, 1), 0)
    for t8 in range(c // 8):
        r8 = slice(8 * t8, 8 * t8 + 8)
        acc = jnp.zeros((8, vf.shape[1]), F32)
        for w in range(per8):
            n = t8 * per8 + w
            mine = (row8 // steps) == w
            s_old = s_in_ref[n]
            part = jnp.dot(qe[r8], s_old.astype(BF16), preferred_element_type=F32)
            acc = jnp.where(mine, part, acc)
            kn = jnp.where(mine, kl[r8], 0.0).astype(BF16)
            upd = lax.dot_general(kn, vb[r8], (((0,), (0,)), ((), ())), preferred_element_type=F32)
            r_last = 8 * t8 + steps * (w + 1) - 1
            edec = _decay_columns(b[r_last:r_last + 1, :])
            dv = upd.shape[1]
            for cb in range(dv // LANES):
                sl = slice(cb * LANES, (cb + 1) * LANES)
                s_ref[n, :, sl] = s_old[:, sl] * edec + upd[:, sl]
        o_t = o[r8] + acc
        o_t = o_t * lax.rsqrt(jnp.mean(o_t * o_t, axis=-1, keepdims=True) + EPS) * gnw_ref[...]
        o_ref[r8, :] = o_t.astype(o_ref.dtype)


def _gla_step(p, gk, gnw, state, steps, d):
    rows = p.shape[0]
    nseq = rows // steps
    dk, dv = d // 2 // GLA_HEADS, d // GLA_HEADS
    assert 8 % steps == 0
    nb = 8
    assert nseq % nb == 0
    c = nb * steps
    qb, kb, vb = 3 * d // dk, 3 * d // dk + GLA_HEADS, 4 * d // dv
    return pl.pallas_call(
        functools.partial(_gla_step_kernel, steps=steps, scale=float(dk) ** -0.5),
        grid=(nseq // nb, GLA_HEADS),
        in_specs=[pl.BlockSpec((c, dk), lambda i, h: (i, qb + h)),
                  pl.BlockSpec((c, dk), lambda i, h: (i, kb + h)),
                  pl.BlockSpec((c, dv), lambda i, h: (i, vb + h)),
                  pl.BlockSpec((c, dk), lambda i, h: (i, h)),
                  pl.BlockSpec((1, dv), lambda i, h: (0, 0)),
                  pl.BlockSpec((nb, None, dk, dv), lambda i, h: (i, h, 0, 0))],
        out_specs=[pl.BlockSpec((c, dv), lambda i, h: (i, h)),
                   pl.BlockSpec((nb, None, dk, dv), lambda i, h: (i, h, 0, 0))],
        out_shape=[jax.ShapeDtypeStruct((rows, d), F32),
                   jax.ShapeDtypeStruct(state.shape, F32)],
        compiler_params=_params("parallel", "parallel"),
        name="gla_step",
    )(p, p, p, gk, gnw.reshape(1, dv), state)


def _merge_kernel(cb_ref, cc_ref, cx_ref, g_ref, ga_ref, gb_ref, o_ref, cw_ref,
                  h1_ref, h2_ref, m_ref, u_ref, *, period):
    tm = cc_ref.shape[0]
    u = cc_ref[...].astype(F32) * cx_ref[...].astype(F32)
    row = lax.broadcasted_iota(jnp.int32, (tm, 1), 0)
    u1 = pltpu.roll(u, 1, axis=0)
    u2 = pltpu.roll(u, 2, axis=0)
    if period:
        t = row % period
        u1 = jnp.where(t >= 1, u1, h1_ref[...])
        u2 = jnp.where(t >= 2, u2, h2_ref[...])
    else:
        first = pl.program_id(1) == 0
        halo = h1_ref[...].astype(F32) * h2_ref[...].astype(F32)
        halo = jnp.where(first, 0.0, halo)
        u1 = jnp.where(row == 0, halo[7:8, :], u1)
        u2 = jnp.where(row == 0, halo[6:7, :], jnp.where(row == 1, halo[7:8, :], u2))
    cw = cw_ref[...]
    conv = u2 * cw[0:1, :] + u1 * cw[1:2, :] + u * cw[2:3, :]
    y_a = cb_ref[...].astype(F32) * conv
    y_b = o_ref[...] * jax.nn.silu(g_ref[...].astype(F32))
    m = jax.nn.sigmoid(ga_ref[...].astype(F32)) * y_a + jax.nn.sigmoid(gb_ref[...].astype(F32)) * y_b
    m_ref[...] = m.astype(BF16)
    if period:
        u_ref[...] = u
    else:
        u_ref[...] = u[tm - 8:, :]


def _merge_seq(p, o, cw, nseq, d):
    rows = p.shape[0]
    length = rows // nseq
    tm = _pick_tile(length, 344)
    nt = length // tm

    def blk(c):
        return pl.BlockSpec((tm, d), lambda n, i: (n * nt + i, c))

    def halo(c):
        return pl.BlockSpec((8, d), lambda n, i: (jnp.maximum((n * nt + i) * (tm // 8) - 1, 0), c))

    return pl.pallas_call(
        functools.partial(_merge_kernel, period=0),
        grid=(nseq, nt),
        in_specs=[blk(0), blk(1), blk(2), blk(5), blk(6), blk(7),
                  pl.BlockSpec((tm, d), lambda n, i: (n * nt + i, 0)),
                  pl.BlockSpec((CONV_W, d), lambda n, i: (0, 0)),
                  halo(1), halo(2)],
        out_specs=[pl.BlockSpec((tm, d), lambda n, i: (n * nt + i, 0)),
                   pl.BlockSpec((8, d), lambda n, i: (n * nt + i, 0))],
        out_shape=[jax.ShapeDtypeStruct((rows, d), BF16),
                   jax.ShapeDtypeStruct((nseq * nt * 8, d), F32)],
        compiler_params=_params("parallel", "parallel"),
        name="merge_seq",
    )(p, p, p, p, p, p, o, cw, p, p)


def _merge_step(p, o, cw, hist1, hist2, steps, d):
    rows = p.shape[0]
    tm = _pick_tile(rows, 256)
    assert tm % steps == 0

    def blk(c):
        return pl.BlockSpec((tm, d), lambda i: (i, c))

    full = pl.BlockSpec((tm, d), lambda i: (i, 0))
    return pl.pallas_call(
        functools.partial(_merge_kernel, period=steps),
        grid=(rows // tm,),
        in_specs=[blk(0), blk(1), blk(2), blk(5), blk(6), blk(7), full,
                  pl.BlockSpec((CONV_W, d), lambda i: (0, 0)), full, full],
        out_specs=[full, full],
        out_shape=[jax.ShapeDtypeStruct((rows, d), BF16),
                   jax.ShapeDtypeStruct((rows, d), F32)],
        compiler_params=_params("parallel"),
        name="merge_step",
    )(p, p, p, p, p, p, o, cw, hist1, hist2)


def _trunk(x, weights, depth, d, *, nseq, state_conv=None, state_gla=None):
    (ln_in_w, ln_in_b, w_in_b, wa1, wa2, conv_w, b_a2, gn_w, w_o_b, ln1_w, ln1_b,
     w1_b, w2_b, ln2_w, ln2_b) = weights
    alpha = (2 * depth) ** 0.25
    rows = x.shape[0]
    steps = rows // nseq
    h, hb = _ln(x, ln_in_w, ln_in_b)
    conv_new, gla_new = [], []
    for l in range(depth):
        gk = _gate(hb, wa1[l], wa2[l], b_a2[l])
        p = _mm(hb, w_in_b[l], relu2=False)
        if state_gla is None:
            o, gs = _gla_seq(p, gk, gn_w[l], nseq, d)
            m, utail = _merge_seq(p, o, conv_w[l], nseq, d)
            nt = utail.shape[0] // (8 * nseq)
            cs = utail.reshape(nseq, nt, 8, d)[:, nt - 1, 8 - (CONV_W - 1):, :]
        else:
            o, gs = _gla_step(p, gk, gn_w[l], state_gla[l], steps, d)
            st = state_conv[l]
            zero = jnp.zeros((nseq, steps - 1, d), F32)
            hist1 = jnp.concatenate([st[:, 1:2], zero], axis=1).reshape(rows, d)
            hist2 = jnp.concatenate([st, zero[:, :steps - 2]], axis=1).reshape(rows, d)
            m, u = _merge_step(p, o, conv_w[l], hist1, hist2, steps, d)
            cs = u.reshape(nseq, steps, d)[:, steps - (CONV_W - 1):, :]
        h, hb = _mm_res_ln(m, w_o_b[l], h, ln1_w[l], ln1_b[l], alpha)
        f = _mm(hb, w1_b[l], relu2=True)
        h, hb = _mm_res_ln(f, w2_b[l], h, ln2_w[l], ln2_b[l], alpha)
        conv_new.append(cs)
        gla_new.append(gs)
    return h, jnp.stack(conv_new), jnp.stack(gla_new)


def kernel(x_prompt, x_sample, state_conv, state_gla, meta, ln_in_w, ln_in_b, w_in, conv_w, w_a2,
           b_a2, gn_w, w_o, ln1_w, ln1_b, w1, w2, ln2_w, ln2_b):
    depth, d, _ = w_in.shape
    nb, seq, _ = x_prompt.shape
    ns, steps, _ = x_sample.shape
    n_meta = meta.shape[0]
    rank = w_a2.shape[1]
    a_lo = 6 * d
    assert w_in.shape[2] == 8 * d + rank and steps >= CONV_W - 1

    w_in_b = jnp.concatenate([w_in[:, :, :a_lo], w_in[:, :, a_lo + rank:]], axis=-1).astype(BF16)
    wa1 = jnp.pad(w_in[:, :, a_lo:a_lo + rank], ((0, 0), (0, 0), (0, LANES - rank))).astype(BF16)
    wa2 = jnp.pad(w_a2, ((0, 0), (0, LANES - rank), (0, 0))).astype(BF16)
    weights = (ln_in_w, ln_in_b, w_in_b, wa1, wa2, conv_w, b_a2, gn_w, w_o.astype(BF16),
               ln1_w, ln1_b, w1.astype(BF16), w2.astype(BF16), ln2_w, ln2_b)

    hp = jnp.concatenate([jnp.broadcast_to(meta.astype(F32), (nb, n_meta, d)), x_prompt], axis=1)
    length = n_meta + seq
    yp, conv_p, gla_p = _trunk(hp.reshape(nb * length, d), weights, depth, d, nseq=nb)
    y_prompt = yp.reshape(nb, length, d)[:, n_meta:]

    ys, conv_s, gla_s = _trunk(x_sample.reshape(ns * steps, d), weights, depth, d, nseq=ns,
                               state_conv=state_conv, state_gla=state_gla)
    y_sample = ys.reshape(ns, steps, d)
    return (y_prompt, y_sample, conv_p, gla_p, conv_s, gla_s)
```

```python
import functools

import jax
import jax.numpy as jnp
from jax import lax
from jax.experimental import pallas as pl
from jax.experimental.pallas import tpu as pltpu

GLA_HEADS = 4
GATE_NORM = 16.0
CONV_W = 3
CHUNK = 64
SUB = 8
EPS = 1e-5
LANES = 128
VMEM_LIMIT = 56 * 1024 * 1024

F32 = jnp.float32
BF16 = jnp.bfloat16


def _params(*sem):
    return pltpu.CompilerParams(dimension_semantics=sem, vmem_limit_bytes=VMEM_LIMIT)


def _pick_tile(n, cap):
    best = None
    for t in range(8, min(n, cap) + 1, 8):
        if n % t == 0:
            best = t
    assert best is not None, (n, cap)
    return best


def _layer_norm(x, w, b):
    mu = jnp.mean(x, axis=-1, keepdims=True)
    xc = x - mu
    var = jnp.mean(xc * xc, axis=-1, keepdims=True)
    return xc * lax.rsqrt(var + EPS) * w + b


def _split3(x):
    hi = x.astype(BF16)
    r1 = x - hi.astype(F32)
    mid = r1.astype(BF16)
    lo = (r1 - mid.astype(F32)).astype(BF16)
    return hi, mid, lo


def _ln_kernel(x_ref, w_ref, b_ref, o_ref, ob_ref):
    y = _layer_norm(x_ref[...], w_ref[...], b_ref[...])
    o_ref[...] = y
    ob_ref[...] = y.astype(BF16)


def _ln(x, w, b):
    r, d = x.shape
    tm = _pick_tile(r, 1024)
    return pl.pallas_call(
        _ln_kernel,
        grid=(r // tm,),
        in_specs=[pl.BlockSpec((tm, d), lambda i: (i, 0)),
                  pl.BlockSpec((1, d), lambda i: (0, 0)),
                  pl.BlockSpec((1, d), lambda i: (0, 0))],
        out_specs=[pl.BlockSpec((tm, d), lambda i: (i, 0)),
                   pl.BlockSpec((tm, d), lambda i: (i, 0))],
        out_shape=[jax.ShapeDtypeStruct((r, d), F32), jax.ShapeDtypeStruct((r, d), BF16)],
        compiler_params=_params("parallel"),
        name="ln_in",
    )(x, w.reshape(1, d), b.reshape(1, d))


def _gate_kernel(hb_ref, wa1_ref, wa2_ref, ba2_ref, gk_ref):
    a1 = jnp.dot(hb_ref[...], wa1_ref[...].astype(BF16), preferred_element_type=F32)
    z = jnp.dot(a1.astype(BF16), wa2_ref[...], preferred_element_type=F32) + ba2_ref[...]
    ls = jnp.minimum(z, 0.0) - jnp.log1p(jnp.exp(-jnp.abs(z)))
    gk_ref[...] = ls / GATE_NORM


def _gate(hb, w_in, wa2, b_a2, l, a_lo):
    r, d = hb.shape
    hk = wa2.shape[2]
    tm = _pick_tile(r, 1024)
    return pl.pallas_call(
        _gate_kernel,
        grid=(r // tm,),
        in_specs=[pl.BlockSpec((tm, d), lambda i: (i, 0)),
                  pl.BlockSpec((None, d, LANES), lambda i: (l, 0, a_lo // LANES)),
                  pl.BlockSpec((None, LANES, hk), lambda i: (l, 0, 0)),
                  pl.BlockSpec((None, 1, hk), lambda i: (l, 0, 0))],
        out_specs=pl.BlockSpec((tm, hk), lambda i: (i, 0)),
        out_shape=jax.ShapeDtypeStruct((r, hk), F32),
        compiler_params=_params("parallel"),
        name="gate",
    )(hb, w_in, wa2, b_a2.reshape(b_a2.shape[0], 1, hk))


W_ROWS = 512


def _cast_weight_tile(wa_ref, wb_ref, wbf_ref, shift):
    k, tn = wa_ref.shape
    for r0 in range(0, k, W_ROWS):
        rows = slice(r0, r0 + W_ROWS)
        wa = wa_ref[rows, :]
        if shift == 0:
            wbf_ref[rows, :] = wa.astype(BF16)
            continue
        ra = pltpu.roll(wa, tn - shift, axis=1)
        rb = pltpu.roll(wb_ref[rows, :], LANES - shift, axis=1)
        lane = lax.broadcasted_iota(jnp.int32, (W_ROWS, LANES), 1)
        wbf_ref[rows, :tn - LANES] = ra[:, :tn - LANES].astype(BF16)
        wbf_ref[rows, tn - LANES:] = jnp.where(lane >= LANES - shift, rb, ra[:, tn - LANES:]).astype(BF16)


def _mm_kernel(x_ref, wa_ref, wb_ref, o_ref, wbf_ref, *, relu2, n_aligned, shift):
    j = pl.program_id(0)

    @pl.when(pl.program_id(1) == 0)
    def _():
        if shift == 0:
            _cast_weight_tile(wa_ref, wb_ref, wbf_ref, 0)
        else:
            @pl.when(j < n_aligned)
            def _():
                _cast_weight_tile(wa_ref, wb_ref, wbf_ref, 0)

            @pl.when(j >= n_aligned)
            def _():
                _cast_weight_tile(wa_ref, wb_ref, wbf_ref, shift)

    acc = jnp.dot(x_ref[...], wbf_ref[...], preferred_element_type=F32)
    if relu2:
        acc = jnp.square(jnp.maximum(acc, 0.0))
    o_ref[...] = acc.astype(o_ref.dtype)


def _mm(x, w, l, n, relu2, skip_lo=0, skip=0):
    r, k = x.shape
    tm = _pick_tile(r, 1032)
    tn = 1024
    assert n % tn == 0 and skip_lo % tn == 0 and skip < LANES and k % W_ROWS == 0
    n_aligned = skip_lo // tn if skip else n // tn
    per = tn // LANES
    first_shifted = (n_aligned + 1) * per

    def wb_map(j, i):
        return (l, 0, jnp.maximum((j + 1) * per, first_shifted))

    return pl.pallas_call(
        functools.partial(_mm_kernel, relu2=relu2, n_aligned=n_aligned, shift=skip),
        grid=(n // tn, r // tm),
        in_specs=[pl.BlockSpec((tm, k), lambda j, i: (i, 0)),
                  pl.BlockSpec((None, k, tn), lambda j, i: (l, 0, j)),
                  pl.BlockSpec((None, k, LANES), wb_map if skip else (lambda j, i: (l, 0, 0)))],
        out_specs=pl.BlockSpec((tm, tn), lambda j, i: (i, j)),
        out_shape=jax.ShapeDtypeStruct((r, n), BF16),
        scratch_shapes=[pltpu.VMEM((k, tn), BF16)],
        compiler_params=_params("arbitrary", "arbitrary"),
        name="mm_relu2" if relu2 else "mm_proj",
    )(x, w, w)


def _mm_res_ln_kernel(a_ref, w_ref, res_ref, lw_ref, lb_ref, o_ref, ob_ref, *, alpha, nk):
    kk = pl.program_id(1)
    part = jnp.dot(a_ref[...], w_ref[...], preferred_element_type=F32)

    @pl.when(kk == 0)
    def _():
        o_ref[...] = part

    @pl.when(kk > 0)
    def _():
        o_ref[...] += part

    @pl.when(kk == nk - 1)
    def _():
        y = _layer_norm(alpha * res_ref[...] + o_ref[...], lw_ref[...], lb_ref[...])
        o_ref[...] = y
        ob_ref[...] = y.astype(BF16)


def _mm_res_ln(a, w, l, res, lw, lb, alpha):
    r, k = a.shape
    d = w.shape[2]
    tm = _pick_tile(r, 688)
    tk = 1024 if k % 1024 == 0 else k
    nk = k // tk
    return pl.pallas_call(
        functools.partial(_mm_res_ln_kernel, alpha=alpha, nk=nk),
        grid=(r // tm, nk),
        in_specs=[pl.BlockSpec((tm, tk), lambda i, kk: (i, kk)),
                  pl.BlockSpec((None, tk, d), lambda i, kk: (l, kk, 0)),
                  pl.BlockSpec((tm, d), lambda i, kk: (i, 0)),
                  pl.BlockSpec((None, 1, d), lambda i, kk: (l, 0, 0)),
                  pl.BlockSpec((None, 1, d), lambda i, kk: (l, 0, 0))],
        out_specs=[pl.BlockSpec((tm, d), lambda i, kk: (i, 0)),
                   pl.BlockSpec((tm, d), lambda i, kk: (i, 0))],
        out_shape=[jax.ShapeDtypeStruct((r, d), F32), jax.ShapeDtypeStruct((r, d), BF16)],
        compiler_params=_params("parallel", "arbitrary"),
        name="mm_res_ln",
    )(a, w, res, lw, lb)


def _cum_matrices(c, group):
    t = lax.broadcasted_iota(jnp.int32, (c, c), 0)
    u = lax.broadcasted_iota(jnp.int32, (c, c), 1)
    same = (t // group) == (u // group)
    tri = same & (u <= t)
    if group == c:
        tend = (u // SUB) <= (t // SUB)
    else:
        tend = same
    m = jnp.concatenate([tri, tend], axis=0).astype(BF16)
    return jnp.concatenate([m, m, m], axis=1)


def _cumsum2(cm, gk):
    c = gk.shape[0]
    hi, mid, lo = _split3(gk)
    g3 = jnp.concatenate([hi, mid, lo], axis=0)
    be = jnp.dot(cm, g3, preferred_element_type=F32)
    return be[:c], be[c:]


def _diag_lags(qf, kf, b, nlag):
    cols = []
    for d in range(nlag):
        if d == 0:
            p = qf * kf
        else:
            kr = pltpu.roll(kf, d, axis=0)
            br = pltpu.roll(b, d, axis=0)
            p = qf * kr * jnp.exp(jnp.minimum(b - br, 0.0))
        cols.append(jnp.sum(p, axis=-1, keepdims=True))
    return cols


def _decay_columns(b_last_row):
    dk = b_last_row.shape[1]
    hi, mid, lo = _split3(b_last_row)
    lm = jnp.concatenate([hi, mid, lo, jnp.zeros((16 - 3, dk), BF16)], axis=0)
    ones = jnp.ones((16, LANES), BF16)
    col = lax.dot_general(lm, ones, (((0,), (0,)), ((), ())), preferred_element_type=F32)
    return jnp.exp(col)


def _state_update(s_ref, edec, upd):
    dv = upd.shape[1]
    for cb in range(dv // LANES):
        sl = slice(cb * LANES, (cb + 1) * LANES)
        s_ref[:, sl] = s_ref[:, sl] * edec + upd[:, sl]


def _gla_chunk(q_ref, k_ref, v_ref, gk_ref, gnw, o_ref, s_ref, r0, c, cm, scale):
    rows = pl.ds(r0, c)
    gk = gk_ref[rows, :]
    qf = q_ref[rows, :].astype(F32) * scale
    kf = k_ref[rows, :].astype(F32)
    vb = v_ref[rows, :]
    nsub = c // SUB

    b, eb = _cumsum2(cm, gk)
    b_last = b[c - 1:c, :]

    s_bf = s_ref[...].astype(BF16)
    o = jnp.dot((qf * jnp.exp(b)).astype(BF16), s_bf, preferred_element_type=F32)

    row = lax.broadcasted_iota(jnp.int32, (c, c), 0)
    lane = lax.broadcasted_iota(jnp.int32, (c, c), 1)
    a = jnp.zeros((c, c), F32)
    for d, col in enumerate(_diag_lags(qf, kf, b, SUB)):
        a = jnp.where((lane == row - d) & ((row % SUB) >= d), col, a)

    khat = kf * jnp.exp(eb - b)
    if nsub > 1:
        pieces = []
        for j in range(nsub - 1):
            lo = SUB * (j + 1)
            pieces.append(qf[lo:] * jnp.exp(b[lo:] - eb[SUB * j:SUB * j + 1, :]))
        qs = jnp.concatenate(pieces, axis=0).astype(BF16)
        prod = lax.dot_general(qs, khat.astype(BF16), (((1,), (1,)), ((), ())),
                               preferred_element_type=F32)
        off = 0
        segs = []
        for j in range(nsub - 1):
            lo = SUB * (j + 1)
            seg = prod[off:off + c - lo, :]
            off += c - lo
            lane_j = lax.broadcasted_iota(jnp.int32, seg.shape, 1)
            seg = jnp.where((lane_j >= SUB * j) & (lane_j < SUB * (j + 1)), seg, 0.0)
            segs.append(jnp.concatenate([jnp.zeros((lo, c), F32), seg], axis=0))
        for seg in segs:
            a = a + seg

    o = o + jnp.dot(a.astype(BF16), vb, preferred_element_type=F32)
    o = o * lax.rsqrt(jnp.mean(o * o, axis=-1, keepdims=True) + EPS) * gnw
    o_ref[rows, :] = o.astype(o_ref.dtype)

    kl = (khat * jnp.exp(b_last - eb)).astype(BF16)
    upd = lax.dot_general(kl, vb, (((0,), (0,)), ((), ())), preferred_element_type=F32)
    _state_update(s_ref, _decay_columns(b_last), upd)


def _gla_seq_kernel(q_ref, k_ref, v_ref, gk_ref, gnw_ref, o_ref, s_ref, *, lead, scale):
    length = q_ref.shape[0]
    gnw = gnw_ref[...]
    s_ref[...] = jnp.zeros_like(s_ref)
    if lead:
        _gla_chunk(q_ref, k_ref, v_ref, gk_ref, gnw, o_ref, s_ref, 0, lead,
                   _cum_matrices(lead, lead), scale)
    n_chunks = (length - lead) // CHUNK
    cm = _cum_matrices(CHUNK, CHUNK)

    def body(ci, carry):
        r0 = pl.multiple_of(lead + ci * CHUNK, SUB)
        _gla_chunk(q_ref, k_ref, v_ref, gk_ref, gnw, o_ref, s_ref, r0, CHUNK, cm, scale)
        return carry

    lax.fori_loop(0, n_chunks, body, 0)


def _drop_ref(fn, idx):
    def wrapped(*refs):
        return fn(*refs[:idx], *refs[idx + 1:])
    return wrapped


def _gla_seq(p, gk, gnw, l, prev_state, depth, nseq, d):
    rows = p.shape[0]
    length = rows // nseq
    dk, dv = d // 2 // GLA_HEADS, d // GLA_HEADS
    lead = (length % CHUNK)
    assert lead % SUB == 0 and length % 8 == 0
    qb, kb, vb = 3 * d // dk, 3 * d // dk + GLA_HEADS, 4 * d // dv
    body = functools.partial(_gla_seq_kernel, lead=lead, scale=float(dk) ** -0.5)
    in_specs = [pl.BlockSpec((length, dk), lambda n, h: (n, qb + h)),
                pl.BlockSpec((length, dk), lambda n, h: (n, kb + h)),
                pl.BlockSpec((length, dv), lambda n, h: (n, vb + h)),
                pl.BlockSpec((length, dk), lambda n, h: (n, h)),
                pl.BlockSpec((None, 1, dv), lambda n, h: (l, 0, 0))]
    args = [p, p, p, gk, gnw]
    aliases = {}
    if prev_state is not None:
        in_specs.append(pl.BlockSpec(memory_space=pl.ANY))
        args.append(prev_state)
        aliases = {5: 1}
        body = _drop_ref(body, 5)
    return pl.pallas_call(
        body,
        grid=(nseq, GLA_HEADS),
        in_specs=in_specs,
        out_specs=[pl.BlockSpec((length, dv), lambda n, h: (n, h)),
                   pl.BlockSpec((None, None, None, dk, dv), lambda n, h: (l, n, h, 0, 0))],
        out_shape=[jax.ShapeDtypeStruct((rows, d), F32),
                   jax.ShapeDtypeStruct((depth, nseq, GLA_HEADS, dk, dv), F32)],
        input_output_aliases=aliases,
        compiler_params=_params("parallel", "parallel"),
        name="gla_seq",
    )(*args)


def _gla_step_kernel(q_ref, k_ref, v_ref, gk_ref, gnw_ref, s_in_ref, o_ref, s_ref, *, steps, scale):
    c = q_ref.shape[0]
    nb = c // steps
    per8 = 8 // steps
    gk = gk_ref[...]
    qf = q_ref[...].astype(F32) * scale
    kf = k_ref[...].astype(F32)
    vb = v_ref[...]
    vf = vb.astype(F32)

    b, eb = _cumsum2(_cum_matrices(c, steps), gk)
    row = lax.broadcasted_iota(jnp.int32, (c, 1), 0)
    o = jnp.zeros(vf.shape, F32)
    for d, col in enumerate(_diag_lags(qf, kf, b, steps)):
        vr = vf if d == 0 else pltpu.roll(vf, d, axis=0)
        o = o + jnp.where((row % steps) >= d, col, 0.0) * vr

    qe = (qf * jnp.exp(b)).astype(BF16)
    kl = kf * jnp.exp(eb - b)
    row8 = lax.broadcasted_iota(jnp.int32, (8, 1), 0)
    for t8 in range(c // 8):
        r8 = slice(8 * t8, 8 * t8 + 8)
        acc = jnp.zeros((8, vf.shape[1]), F32)
        for w in range(per8):
            n = t8 * per8 + w
            mine = (row8 // steps) == w
            s_old = s_in_ref[n]
            part = jnp.dot(qe[r8], s_old.astype(BF16), preferred_element_type=F32)
            acc = jnp.where(mine, part, acc)
            kn = jnp.where(mine, kl[r8], 0.0).astype(BF16)
            upd = lax.dot_general(kn, vb[r8], (((0,), (0,)), ((), ())), preferred_element_type=F32)
            r_last = 8 * t8 + steps * (w + 1) - 1
            edec = _decay_columns(b[r_last:r_last + 1, :])
            dv = upd.shape[1]
            for cb in range(dv // LANES):
                sl = slice(cb * LANES, (cb + 1) * LANES)
                s_ref[n, :, sl] = s_old[:, sl] * edec + upd[:, sl]
        o_t = o[r8] + acc
        o_t = o_t * lax.rsqrt(jnp.mean(o_t * o_t, axis=-1, keepdims=True) + EPS) * gnw_ref[...]
        o_ref[r8, :] = o_t.astype(o_ref.dtype)


def _gla_step(p, gk, gnw, state, l, prev_state, steps, d):
    rows = p.shape[0]
    nseq = rows // steps
    dk, dv = d // 2 // GLA_HEADS, d // GLA_HEADS
    assert 8 % steps == 0
    nb = 8
    assert nseq % nb == 0
    c = nb * steps
    qb, kb, vb = 3 * d // dk, 3 * d // dk + GLA_HEADS, 4 * d // dv
    body = functools.partial(_gla_step_kernel, steps=steps, scale=float(dk) ** -0.5)
    state_spec = pl.BlockSpec((None, nb, None, dk, dv), lambda i, h: (l, i, h, 0, 0))
    in_specs = [pl.BlockSpec((c, dk), lambda i, h: (i, qb + h)),
                pl.BlockSpec((c, dk), lambda i, h: (i, kb + h)),
                pl.BlockSpec((c, dv), lambda i, h: (i, vb + h)),
                pl.BlockSpec((c, dk), lambda i, h: (i, h)),
                pl.BlockSpec((None, 1, dv), lambda i, h: (l, 0, 0)),
                state_spec]
    args = [p, p, p, gk, gnw, state]
    aliases = {}
    if prev_state is not None:
        in_specs.append(pl.BlockSpec(memory_space=pl.ANY))
        args.append(prev_state)
        aliases = {6: 1}
        body = _drop_ref(body, 6)
    return pl.pallas_call(
        body,
        grid=(nseq // nb, GLA_HEADS),
        in_specs=in_specs,
        out_specs=[pl.BlockSpec((c, dv), lambda i, h: (i, h)), state_spec],
        out_shape=[jax.ShapeDtypeStruct((rows, d), F32),
                   jax.ShapeDtypeStruct(state.shape, F32)],
        input_output_aliases=aliases,
        compiler_params=_params("parallel", "parallel"),
        name="gla_step",
    )(*args)


def _merge_kernel(cb_ref, cc_ref, cx_ref, g_ref, ga_ref, gb_ref, o_ref, cw_ref,
                  h1_ref, h2_ref, m_ref, u_ref, *, period):
    tm = cc_ref.shape[0]
    u = cc_ref[...].astype(F32) * cx_ref[...].astype(F32)
    row = lax.broadcasted_iota(jnp.int32, (tm, 1), 0)
    u1 = pltpu.roll(u, 1, axis=0)
    u2 = pltpu.roll(u, 2, axis=0)
    if period:
        t = row % period
        u1 = jnp.where(t >= 1, u1, h1_ref[...])
        u2 = jnp.where(t >= 2, u2, h2_ref[...])
    else:
        first = pl.program_id(1) == 0
        halo = h1_ref[...].astype(F32) * h2_ref[...].astype(F32)
        halo = jnp.where(first, 0.0, halo)
        u1 = jnp.where(row == 0, halo[7:8, :], u1)
        u2 = jnp.where(row == 0, halo[6:7, :], jnp.where(row == 1, halo[7:8, :], u2))
    cw = cw_ref[...]
    conv = u2 * cw[0:1, :] + u1 * cw[1:2, :] + u * cw[2:3, :]
    y_a = cb_ref[...].astype(F32) * conv
    y_b = o_ref[...] * jax.nn.silu(g_ref[...].astype(F32))
    m = jax.nn.sigmoid(ga_ref[...].astype(F32)) * y_a + jax.nn.sigmoid(gb_ref[...].astype(F32)) * y_b
    m_ref[...] = m.astype(BF16)
    if period:
        u_ref[...] = u
    else:
        u_ref[...] = u[tm - 8:, :]


def _merge_seq(p, o, cw, l, nseq, d):
    rows = p.shape[0]
    length = rows // nseq
    tm = _pick_tile(length, 344)
    nt = length // tm

    def blk(c):
        return pl.BlockSpec((tm, d), lambda n, i: (n * nt + i, c))

    def halo(c):
        return pl.BlockSpec((8, d), lambda n, i: (jnp.maximum((n * nt + i) * (tm // 8) - 1, 0), c))

    return pl.pallas_call(
        functools.partial(_merge_kernel, period=0),
        grid=(nseq, nt),
        in_specs=[blk(0), blk(1), blk(2), blk(5), blk(6), blk(7),
                  pl.BlockSpec((tm, d), lambda n, i: (n * nt + i, 0)),
                  pl.BlockSpec((None, CONV_W, d), lambda n, i: (l, 0, 0)),
                  halo(1), halo(2)],
        out_specs=[pl.BlockSpec((tm, d), lambda n, i: (n * nt + i, 0)),
                   pl.BlockSpec((8, d), lambda n, i: (n * nt + i, 0))],
        out_shape=[jax.ShapeDtypeStruct((rows, d), BF16),
                   jax.ShapeDtypeStruct((nseq * nt * 8, d), F32)],
        compiler_params=_params("parallel", "parallel"),
        name="merge_seq",
    )(p, p, p, p, p, p, o, cw, p, p)


def _merge_step(p, o, cw, l, hist1, hist2, steps, d):
    rows = p.shape[0]
    tm = _pick_tile(rows, 256)
    assert tm % steps == 0

    def blk(c):
        return pl.BlockSpec((tm, d), lambda i: (i, c))

    full = pl.BlockSpec((tm, d), lambda i: (i, 0))
    return pl.pallas_call(
        functools.partial(_merge_kernel, period=steps),
        grid=(rows // tm,),
        in_specs=[blk(0), blk(1), blk(2), blk(5), blk(6), blk(7), full,
                  pl.BlockSpec((None, CONV_W, d), lambda i: (l, 0, 0)), full, full],
        out_specs=[full, full],
        out_shape=[jax.ShapeDtypeStruct((rows, d), BF16),
                   jax.ShapeDtypeStruct((rows, d), F32)],
        compiler_params=_params("parallel"),
        name="merge_step",
    )(p, p, p, p, p, p, o, cw, hist1, hist2)


def _trunk(x, weights, depth, d, a_lo, rank, *, nseq, state_conv=None, state_gla=None):
    (ln_in_w, ln_in_b, w_in, wa2, conv_w, b_a2, gn_w, w_o_b, ln1_w, ln1_b,
     w1, w2_b, ln2_w, ln2_b) = weights
    alpha = (2 * depth) ** 0.25
    rows = x.shape[0]
    steps = rows // nseq
    h, hb = _ln(x, ln_in_w, ln_in_b)
    conv_new, gs = [], None
    for l in range(depth):
        gk = _gate(hb, w_in, wa2, b_a2, l, a_lo)
        p = _mm(hb, w_in, l, 8 * d, relu2=False, skip_lo=a_lo, skip=rank)
        if state_gla is None:
            o, gs = _gla_seq(p, gk, gn_w, l, gs, depth, nseq, d)
            m, utail = _merge_seq(p, o, conv_w, l, nseq, d)
            nt = utail.shape[0] // (8 * nseq)
            cs = utail.reshape(nseq, nt, 8, d)[:, nt - 1, 8 - (CONV_W - 1):, :]
        else:
            o, gs = _gla_step(p, gk, gn_w, state_gla, l, gs, steps, d)
            st = state_conv[l]
            zero = jnp.zeros((nseq, steps - 1, d), F32)
            hist1 = jnp.concatenate([st[:, 1:2], zero], axis=1).reshape(rows, d)
            hist2 = jnp.concatenate([st, zero[:, :steps - 2]], axis=1).reshape(rows, d)
            m, u = _merge_step(p, o, conv_w, l, hist1, hist2, steps, d)
            cs = u.reshape(nseq, steps, d)[:, steps - (CONV_W - 1):, :]
        h, hb = _mm_res_ln(m, w_o_b, l, h, ln1_w, ln1_b, alpha)
        f = _mm(hb, w1, l, w1.shape[2], relu2=True)
        h, hb = _mm_res_ln(f, w2_b, l, h, ln2_w, ln2_b, alpha)
        conv_new.append(cs)
    return h, jnp.stack(conv_new), gs


def kernel(x_prompt, x_sample, state_conv, state_gla, meta, ln_in_w, ln_in_b, w_in, conv_w, w_a2,
           b_a2, gn_w, w_o, ln1_w, ln1_b, w1, w2, ln2_w, ln2_b):
    depth, d, _ = w_in.shape
    nb, seq, _ = x_prompt.shape
    ns, steps, _ = x_sample.shape
    n_meta = meta.shape[0]
    rank = w_a2.shape[1]
    a_lo = 6 * d
    assert w_in.shape[2] == 8 * d + rank and steps >= CONV_W - 1 and rank < LANES

    def per_layer(v):
        return v.reshape(depth, 1, v.shape[-1])

    wa2 = jnp.pad(w_a2, ((0, 0), (0, LANES - rank), (0, 0))).astype(BF16)
    weights = (ln_in_w, ln_in_b, w_in, wa2, conv_w, b_a2, per_layer(gn_w), w_o.astype(BF16),
               per_layer(ln1_w), per_layer(ln1_b), w1, w2.astype(BF16),
               per_layer(ln2_w), per_layer(ln2_b))

    hp = jnp.concatenate([jnp.broadcast_to(meta.astype(F32), (nb, n_meta, d)), x_prompt], axis=1)
    length = n_meta + seq
    yp, conv_p, gla_p = _trunk(hp.reshape(nb * length, d), weights, depth, d, a_lo, rank, nseq=nb)
    y_prompt = yp.reshape(nb, length, d)[:, n_meta:]

    ys, conv_s, gla_s = _trunk(x_sample.reshape(ns * steps, d), weights, depth, d, a_lo, rank,
                               nseq=ns, state_conv=state_conv, state_gla=state_gla)
    y_sample = ys.reshape(ns, steps, d)
    return (y_prompt, y_sample, conv_p, gla_p, conv_s, gla_s)
```

```python
import functools

import jax
import jax.numpy as jnp
from jax import lax
from jax.experimental import pallas as pl
from jax.experimental.pallas import tpu as pltpu

GLA_HEADS = 4
HEADS_PER_STEP = 2
GATE_NORM = 16.0
CONV_W = 3
CHUNK = 64
SUB = 8
EPS = 1e-5
LOG2E = 1.4426950408889634
LANES = 128
VMEM_LIMIT = 56 * 1024 * 1024

F32 = jnp.float32
BF16 = jnp.bfloat16


def _params(*sem):
    return pltpu.CompilerParams(dimension_semantics=sem, vmem_limit_bytes=VMEM_LIMIT)


def _pick_tile(n, cap):
    best = None
    for t in range(8, min(n, cap) + 1, 8):
        if n % t == 0:
            best = t
    assert best is not None, (n, cap)
    return best


def _layer_norm(x, w, b):
    mu = jnp.mean(x, axis=-1, keepdims=True)
    xc = x - mu
    var = jnp.mean(xc * xc, axis=-1, keepdims=True)
    return xc * lax.rsqrt(var + EPS) * w + b


def _split3(x):
    hi = x.astype(BF16)
    r1 = x - hi.astype(F32)
    mid = r1.astype(BF16)
    lo = (r1 - mid.astype(F32)).astype(BF16)
    return hi, mid, lo


def _ln_kernel(x_ref, w_ref, b_ref, o_ref, ob_ref):
    y = _layer_norm(x_ref[...], w_ref[...], b_ref[...])
    o_ref[...] = y
    ob_ref[...] = y.astype(BF16)


def _ln(x, w, b):
    r, d = x.shape
    tm = _pick_tile(r, 1024)
    return pl.pallas_call(
        _ln_kernel,
        grid=(r // tm,),
        in_specs=[pl.BlockSpec((tm, d), lambda i: (i, 0)),
                  pl.BlockSpec((1, d), lambda i: (0, 0)),
                  pl.BlockSpec((1, d), lambda i: (0, 0))],
        out_specs=[pl.BlockSpec((tm, d), lambda i: (i, 0)),
                   pl.BlockSpec((tm, d), lambda i: (i, 0))],
        out_shape=[jax.ShapeDtypeStruct((r, d), F32), jax.ShapeDtypeStruct((r, d), BF16)],
        compiler_params=_params("parallel"),
        name="ln_in",
    )(x, w.reshape(1, d), b.reshape(1, d))


def _gate_kernel(hb_ref, wa1_ref, wa2_ref, ba2_ref, gk_ref):
    a1 = lax.dot_general(hb_ref[...], wa1_ref[...].astype(BF16), (((1,), (1,)), ((), ())),
                         preferred_element_type=F32)
    z = jnp.dot(a1.astype(BF16), wa2_ref[...], preferred_element_type=F32) + ba2_ref[...]
    ls = jnp.minimum(z, 0.0) - jnp.log1p(jnp.exp(-jnp.abs(z)))
    gk_ref[...] = ls * (LOG2E / GATE_NORM)


def _gate(hb, w_in_t, wa2, b_a2, l, a_lo):
    r, d = hb.shape
    hk = wa2.shape[2]
    tm = _pick_tile(r, 1024)
    return pl.pallas_call(
        _gate_kernel,
        grid=(r // tm,),
        in_specs=[pl.BlockSpec((tm, d), lambda i: (i, 0)),
                  pl.BlockSpec((None, LANES, d), lambda i: (l, a_lo // LANES, 0)),
                  pl.BlockSpec((None, LANES, hk), lambda i: (l, 0, 0)),
                  pl.BlockSpec((None, 1, hk), lambda i: (l, 0, 0))],
        out_specs=pl.BlockSpec((tm, hk), lambda i: (i, 0)),
        out_shape=jax.ShapeDtypeStruct((r, hk), F32),
        compiler_params=_params("parallel"),
        name="gate",
    )(hb, w_in_t, wa2, b_a2.reshape(b_a2.shape[0], 1, hk))


W_ROWS = 512


def _mm_kernel(x_ref, w_ref, o_ref, wbf_ref, *, relu2):
    @pl.when(pl.program_id(1) == 0)
    def _():
        for r0 in range(0, w_ref.shape[0], W_ROWS):
            wbf_ref[r0:r0 + W_ROWS, :] = w_ref[r0:r0 + W_ROWS, :].astype(BF16)

    acc = jnp.dot(x_ref[...], wbf_ref[...], preferred_element_type=F32)
    if relu2:
        acc = jnp.square(jnp.maximum(acc, 0.0))
    o_ref[...] = acc.astype(o_ref.dtype)


def _mm(x, w, l, relu2):
    r, k = x.shape
    n = w.shape[2]
    tm = _pick_tile(r, 1032)
    tn = 1024
    assert n % tn == 0 and k % W_ROWS == 0
    return pl.pallas_call(
        functools.partial(_mm_kernel, relu2=relu2),
        grid=(n // tn, r // tm),
        in_specs=[pl.BlockSpec((tm, k), lambda j, i: (i, 0)),
                  pl.BlockSpec((None, k, tn), lambda j, i: (l, 0, j))],
        out_specs=pl.BlockSpec((tm, tn), lambda j, i: (i, j)),
        out_shape=jax.ShapeDtypeStruct((r, n), BF16),
        scratch_shapes=[pltpu.VMEM((k, tn), BF16)],
        compiler_params=_params("arbitrary", "arbitrary"),
        name="mm_relu2" if relu2 else "mm",
    )(x, w)


def _mmt_kernel(x_ref, wa_ref, wb_ref, o_ref, wbf_ref, *, n_aligned, skip):
    j = pl.program_id(0)
    tn = wa_ref.shape[0]

    @pl.when(pl.program_id(1) == 0)
    def _():
        @pl.when(j < n_aligned)
        def _():
            for r0 in range(0, tn, W_ROWS):
                wbf_ref[r0:r0 + W_ROWS, :] = wa_ref[r0:r0 + W_ROWS, :].astype(BF16)

        @pl.when(j >= n_aligned)
        def _():
            for r0 in range(0, tn, W_ROWS):
                hi = min(r0 + W_ROWS, tn - skip)
                wbf_ref[r0:hi, :] = wa_ref[r0 + skip:hi + skip, :].astype(BF16)
            wbf_ref[tn - skip:, :] = wb_ref[...].astype(BF16)

    acc = lax.dot_general(x_ref[...], wbf_ref[...], (((1,), (1,)), ((), ())),
                          preferred_element_type=F32)
    o_ref[...] = acc.astype(o_ref.dtype)


def _mmt(x, w_t, l, n, skip_lo, skip):
    r, k = x.shape
    tm = _pick_tile(r, 1032)
    tn = 1024
    assert n % tn == 0 and skip_lo % tn == 0 and skip % 16 == 0 and tn % skip == 0
    assert w_t.shape[1] == n + skip and tn % W_ROWS == 0
    n_aligned = skip_lo // tn
    per = tn // skip
    return pl.pallas_call(
        functools.partial(_mmt_kernel, n_aligned=n_aligned, skip=skip),
        grid=(n // tn, r // tm),
        in_specs=[pl.BlockSpec((tm, k), lambda j, i: (i, 0)),
                  pl.BlockSpec((None, tn, k), lambda j, i: (l, j, 0)),
                  pl.BlockSpec((None, skip, k),
                               lambda j, i: (l, jnp.maximum(j, n_aligned) * per + per, 0))],
        out_specs=pl.BlockSpec((tm, tn), lambda j, i: (i, j)),
        out_shape=jax.ShapeDtypeStruct((r, n), BF16),
        scratch_shapes=[pltpu.VMEM((tn, k), BF16)],
        compiler_params=_params("arbitrary", "arbitrary"),
        name="mm_proj",
    )(x, w_t, w_t)


def _mm_res_ln_kernel(a_ref, w_ref, res_ref, lw_ref, lb_ref, o_ref, ob_ref, *, alpha, nk):
    kk = pl.program_id(1)
    part = jnp.dot(a_ref[...], w_ref[...], preferred_element_type=F32)

    def finish(acc):
        y = _layer_norm(alpha * res_ref[...] + acc, lw_ref[...], lb_ref[...])
        o_ref[...] = y
        ob_ref[...] = y.astype(BF16)

    if nk == 1:
        finish(part)
        return

    @pl.when(kk == 0)
    def _():
        o_ref[...] = part

    @pl.when((kk > 0) & (kk < nk - 1))
    def _():
        o_ref[...] += part

    @pl.when(kk == nk - 1)
    def _():
        finish(o_ref[...] + part)


def _mm_res_ln(a, w, l, res, lw, lb, alpha):
    r, k = a.shape
    d = w.shape[2]
    tk = 1024 if k > 2048 and k % 1024 == 0 else k
    nk = k // tk
    tm = _pick_tile(r, 688 if nk > 1 else 344)
    return pl.pallas_call(
        functools.partial(_mm_res_ln_kernel, alpha=alpha, nk=nk),
        grid=(r // tm, nk),
        in_specs=[pl.BlockSpec((tm, tk), lambda i, kk: (i, kk)),
                  pl.BlockSpec((None, tk, d), lambda i, kk: (l, kk, 0)),
                  pl.BlockSpec((tm, d), lambda i, kk: (i, 0)),
                  pl.BlockSpec((None, 1, d), lambda i, kk: (l, 0, 0)),
                  pl.BlockSpec((None, 1, d), lambda i, kk: (l, 0, 0))],
        out_specs=[pl.BlockSpec((tm, d), lambda i, kk: (i, 0)),
                   pl.BlockSpec((tm, d), lambda i, kk: (i, 0))],
        out_shape=[jax.ShapeDtypeStruct((r, d), F32), jax.ShapeDtypeStruct((r, d), BF16)],
        compiler_params=_params("parallel", "arbitrary"),
        name="mm_res_ln",
    )(a, w, res, lw, lb)


def _cum_matrices(c, group):
    t = lax.broadcasted_iota(jnp.int32, (c, c), 0)
    u = lax.broadcasted_iota(jnp.int32, (c, c), 1)
    same = (t // group) == (u // group)
    tri = same & (u <= t)
    if group == c:
        tend = (u // SUB) <= (t // SUB)
    else:
        tend = same
    m = jnp.concatenate([tri, tend], axis=0).astype(BF16)
    return jnp.concatenate([m, m, m], axis=1)


def _cumsum2(cm, gk):
    c = gk.shape[0]
    hi, mid, lo = _split3(gk)
    g3 = jnp.concatenate([hi, mid, lo], axis=0)
    be = jnp.dot(cm, g3, preferred_element_type=F32)
    return be[:c], be[c:]


def _tiles(x):
    return x.reshape(x.shape[0] // 8, 8, x.shape[1])


def _diag_lags(q3, k3, b3, nlag):
    cols = []
    for d in range(nlag):
        if d == 0:
            p = q3 * k3
        else:
            kr = pltpu.roll(k3, d, axis=1)
            br = pltpu.roll(b3, d, axis=1)
            p = q3 * kr * jnp.exp2(jnp.minimum(b3 - br, 0.0))
        cols.append(jnp.sum(p, axis=-1, keepdims=True))
    return cols


def _decay_columns(b_last_row):
    dk = b_last_row.shape[1]
    hi, mid, lo = _split3(b_last_row)
    lm = jnp.concatenate([hi, mid, lo, jnp.zeros((16 - 3, dk), BF16)], axis=0)
    ones = jnp.ones((16, LANES), BF16)
    col = lax.dot_general(lm, ones, (((0,), (0,)), ((), ())), preferred_element_type=F32)
    return jnp.exp2(col)


def _gla_chunk(q_ref, k_ref, v_ref, gk_ref, gnw, o_ref, s_ref, r0, c, cm, scale, hh):
    dk, dv = s_ref.shape[1], s_ref.shape[2]
    rows = pl.ds(r0, c)
    kcols = slice(hh * dk, (hh + 1) * dk)
    vcols = slice(hh * dv, (hh + 1) * dv)
    gk = gk_ref[rows, kcols]
    qf = q_ref[rows, kcols].astype(F32) * scale
    kf = k_ref[rows, kcols].astype(F32)
    vb = v_ref[rows, vcols]
    nsub = c // SUB

    b, eb = _cumsum2(cm, gk)
    b_last = b[c - 1:c, :]

    s_bf = s_ref[hh].astype(BF16)
    o = jnp.dot((qf * jnp.exp2(b)).astype(BF16), s_bf, preferred_element_type=F32)

    g = c // 8
    sub = lax.broadcasted_iota(jnp.int32, (g, 8, c), 1)
    row = 8 * lax.broadcasted_iota(jnp.int32, (g, 8, c), 0) + sub
    lane = lax.broadcasted_iota(jnp.int32, (g, 8, c), 2)
    a3 = jnp.zeros((g, 8, c), F32)
    for d, col in enumerate(_diag_lags(_tiles(qf), _tiles(kf), _tiles(b), SUB)):
        a3 = jnp.where((lane == row - d) & (sub >= d), col, a3)
    a = a3.reshape(c, c)

    khat = kf * jnp.exp2(eb - b)
    if nsub > 1:
        pieces = []
        for j in range(nsub - 1):
            lo = SUB * (j + 1)
            pieces.append(qf[lo:] * jnp.exp2(b[lo:] - eb[SUB * j:SUB * j + 1, :]))
        qs = jnp.concatenate(pieces, axis=0).astype(BF16)
        prod = lax.dot_general(qs, khat.astype(BF16), (((1,), (1,)), ((), ())),
                               preferred_element_type=F32)
        off = 0
        for j in range(nsub - 1):
            lo = SUB * (j + 1)
            seg = prod[off:off + c - lo, :]
            off += c - lo
            lane_j = lax.broadcasted_iota(jnp.int32, seg.shape, 1)
            seg = jnp.where((lane_j >= SUB * j) & (lane_j < SUB * (j + 1)), seg, 0.0)
            a = a + jnp.concatenate([jnp.zeros((lo, c), F32), seg], axis=0)

    o = o + jnp.dot(a.astype(BF16), vb, preferred_element_type=F32)
    o = o * lax.rsqrt(jnp.mean(o * o, axis=-1, keepdims=True) + EPS) * gnw
    o_ref[rows, vcols] = o.astype(o_ref.dtype)

    kl = (khat * jnp.exp2(b_last - eb)).astype(BF16)
    upd = lax.dot_general(kl, vb, (((0,), (0,)), ((), ())), preferred_element_type=F32)
    edec = _decay_columns(b_last)
    for cb in range(dv // LANES):
        sl = slice(cb * LANES, (cb + 1) * LANES)
        s_ref[hh, :, sl] = s_ref[hh, :, sl] * edec + upd[:, sl]


def _gla_seq_kernel(q_ref, k_ref, v_ref, gk_ref, gnw_ref, o_ref, s_ref, *, lead, scale):
    length = q_ref.shape[0]
    gnw = gnw_ref[...]
    s_ref[...] = jnp.zeros_like(s_ref)
    if lead:
        cm_lead = _cum_matrices(lead, lead)
        for hh in range(HEADS_PER_STEP):
            _gla_chunk(q_ref, k_ref, v_ref, gk_ref, gnw, o_ref, s_ref, 0, lead, cm_lead, scale, hh)
    n_chunks = (length - lead) // CHUNK
    cm = _cum_matrices(CHUNK, CHUNK)

    def body(ci, carry):
        r0 = pl.multiple_of(lead + ci * CHUNK, SUB)
        for hh in range(HEADS_PER_STEP):
            _gla_chunk(q_ref, k_ref, v_ref, gk_ref, gnw, o_ref, s_ref, r0, CHUNK, cm, scale, hh)
        return carry

    lax.fori_loop(0, n_chunks, body, 0)


def _drop_ref(fn, idx):
    def wrapped(*refs):
        return fn(*refs[:idx], *refs[idx + 1:])
    return wrapped


def _gla_seq(p, gk, gnw, l, prev_state, depth, nseq, d):
    rows = p.shape[0]
    length = rows // nseq
    dk, dv = d // 2 // GLA_HEADS, d // GLA_HEADS
    hp = HEADS_PER_STEP
    lead = (length % CHUNK)
    assert lead % SUB == 0 and length % 8 == 0 and GLA_HEADS % hp == 0
    qb, kb, vb = 3 * d // (hp * dk), (3 * d + GLA_HEADS * dk) // (hp * dk), 4 * d // (hp * dv)
    body = functools.partial(_gla_seq_kernel, lead=lead, scale=float(dk) ** -0.5)
    in_specs = [pl.BlockSpec((length, hp * dk), lambda n, h: (n, qb + h)),
                pl.BlockSpec((length, hp * dk), lambda n, h: (n, kb + h)),
                pl.BlockSpec((length, hp * dv), lambda n, h: (n, vb + h)),
                pl.BlockSpec((length, hp * dk), lambda n, h: (n, h)),
                pl.BlockSpec((None, 1, dv), lambda n, h: (l, 0, 0))]
    args = [p, p, p, gk, gnw]
    aliases = {}
    if prev_state is not None:
        in_specs.append(pl.BlockSpec(memory_space=pl.ANY))
        args.append(prev_state)
        aliases = {5: 1}
        body = _drop_ref(body, 5)
    return pl.pallas_call(
        body,
        grid=(nseq, GLA_HEADS // hp),
        in_specs=in_specs,
        out_specs=[pl.BlockSpec((length, hp * dv), lambda n, h: (n, h)),
                   pl.BlockSpec((None, None, hp, dk, dv), lambda n, h: (l, n, h, 0, 0))],
        out_shape=[jax.ShapeDtypeStruct((rows, d), BF16),
                   jax.ShapeDtypeStruct((depth, nseq, GLA_HEADS, dk, dv), F32)],
        input_output_aliases=aliases,
        compiler_params=_params("parallel", "parallel"),
        name="gla_seq",
    )(*args)


def _gla_step_kernel(q_ref, k_ref, v_ref, gk_ref, gnw_ref, s_in_ref, o_ref, s_ref, *, steps, scale):
    c = q_ref.shape[0]
    per8 = 8 // steps
    gk = gk_ref[...]
    qf = q_ref[...].astype(F32) * scale
    kf = k_ref[...].astype(F32)
    vb = v_ref[...]
    v3 = _tiles(vb.astype(F32))

    b, eb = _cumsum2(_cum_matrices(c, steps), gk)
    sub = lax.broadcasted_iota(jnp.int32, (c // 8, 8, 1), 1)
    o3 = jnp.zeros(v3.shape, F32)
    for d, col in enumerate(_diag_lags(_tiles(qf), _tiles(kf), _tiles(b), steps)):
        vr = v3 if d == 0 else pltpu.roll(v3, d, axis=1)
        o3 = o3 + jnp.where((sub % steps) >= d, col, 0.0) * vr
    o = o3.reshape(c, v3.shape[2])

    qe = (qf * jnp.exp2(b)).astype(BF16)
    kl = kf * jnp.exp2(eb - b)
    row8 = lax.broadcasted_iota(jnp.int32, (8, 1), 0)
    for t8 in range(c // 8):
        r8 = slice(8 * t8, 8 * t8 + 8)
        acc = jnp.zeros((8, o.shape[1]), F32)
        for w in range(per8):
            n = t8 * per8 + w
            mine = (row8 // steps) == w
            s_old = s_in_ref[n]
            part = jnp.dot(qe[r8], s_old.astype(BF16), preferred_element_type=F32)
            acc = jnp.where(mine, part, acc)
            kn = jnp.where(mine, kl[r8], 0.0).astype(BF16)
            upd = lax.dot_general(kn, vb[r8], (((0,), (0,)), ((), ())), preferred_element_type=F32)
            r_last = 8 * t8 + steps * (w + 1) - 1
            edec = _decay_columns(b[r_last:r_last + 1, :])
            dv = upd.shape[1]
            for cb in range(dv // LANES):
                sl = slice(cb * LANES, (cb + 1) * LANES)
                s_ref[n, :, sl] = s_old[:, sl] * edec + upd[:, sl]
        o_t = o[r8] + acc
        o_t = o_t * lax.rsqrt(jnp.mean(o_t * o_t, axis=-1, keepdims=True) + EPS) * gnw_ref[...]
        o_ref[r8, :] = o_t.astype(o_ref.dtype)


def _gla_step(p, gk, gnw, state, l, prev_state, steps, d):
    rows = p.shape[0]
    nseq = rows // steps
    dk, dv = d // 2 // GLA_HEADS, d // GLA_HEADS
    assert 8 % steps == 0
    nb = 8
    assert nseq % nb == 0
    c = nb * steps
    qb, kb, vb = 3 * d // dk, 3 * d // dk + GLA_HEADS, 4 * d // dv
    body = functools.partial(_gla_step_kernel, steps=steps, scale=float(dk) ** -0.5)
    state_spec = pl.BlockSpec((None, nb, None, dk, dv), lambda i, h: (l, i, h, 0, 0))
    in_specs = [pl.BlockSpec((c, dk), lambda i, h: (i, qb + h)),
                pl.BlockSpec((c, dk), lambda i, h: (i, kb + h)),
                pl.BlockSpec((c, dv), lambda i, h: (i, vb + h)),
                pl.BlockSpec((c, dk), lambda i, h: (i, h)),
                pl.BlockSpec((None, 1, dv), lambda i, h: (l, 0, 0)),
                state_spec]
    args = [p, p, p, gk, gnw, state]
    aliases = {}
    if prev_state is not None:
        in_specs.append(pl.BlockSpec(memory_space=pl.ANY))
        args.append(prev_state)
        aliases = {6: 1}
        body = _drop_ref(body, 6)
    return pl.pallas_call(
        body,
        grid=(nseq // nb, GLA_HEADS),
        in_specs=in_specs,
        out_specs=[pl.BlockSpec((c, dv), lambda i, h: (i, h)), state_spec],
        out_shape=[jax.ShapeDtypeStruct((rows, d), BF16),
                   jax.ShapeDtypeStruct(state.shape, F32)],
        input_output_aliases=aliases,
        compiler_params=_params("parallel", "parallel"),
        name="gla_step",
    )(*args)


def _merge_kernel(cb_ref, cc_ref, cx_ref, g_ref, ga_ref, gb_ref, o_ref, cw_ref,
                  h1_ref, h2_ref, m_ref, u_ref, *, period):
    tm = cc_ref.shape[0]
    u = cc_ref[...].astype(F32) * cx_ref[...].astype(F32)
    row = lax.broadcasted_iota(jnp.int32, (tm, 1), 0)
    u1 = pltpu.roll(u, 1, axis=0)
    u2 = pltpu.roll(u, 2, axis=0)
    if period:
        t = row % period
        u1 = jnp.where(t >= 1, u1, h1_ref[...])
        u2 = jnp.where(t >= 2, u2, h2_ref[...])
    else:
        first = pl.program_id(1) == 0
        halo = h1_ref[...].astype(F32) * h2_ref[...].astype(F32)
        halo = jnp.where(first, 0.0, halo)
        u1 = jnp.where(row == 0, halo[7:8, :], u1)
        u2 = jnp.where(row == 0, halo[6:7, :], jnp.where(row == 1, halo[7:8, :], u2))
    cw = cw_ref[...]
    conv = u2 * cw[0:1, :] + u1 * cw[1:2, :] + u * cw[2:3, :]
    y_a = cb_ref[...].astype(F32) * conv
    y_b = o_ref[...].astype(F32) * jax.nn.silu(g_ref[...].astype(F32))
    m = jax.nn.sigmoid(ga_ref[...].astype(F32)) * y_a + jax.nn.sigmoid(gb_ref[...].astype(F32)) * y_b
    m_ref[...] = m.astype(BF16)
    if period:
        u_ref[...] = u
    else:
        u_ref[...] = u[tm - 8:, :]


def _merge_seq(p, o, cw, l, nseq, d):
    rows = p.shape[0]
    length = rows // nseq
    tm = _pick_tile(length, 344)
    nt = length // tm

    def blk(c):
        return pl.BlockSpec((tm, d), lambda n, i: (n * nt + i, c))

    def halo(c):
        return pl.BlockSpec((8, d), lambda n, i: (jnp.maximum((n * nt + i) * (tm // 8) - 1, 0), c))

    return pl.pallas_call(
        functools.partial(_merge_kernel, period=0),
        grid=(nseq, nt),
        in_specs=[blk(0), blk(1), blk(2), blk(5), blk(6), blk(7),
                  pl.BlockSpec((tm, d), lambda n, i: (n * nt + i, 0)),
                  pl.BlockSpec((None, CONV_W, d), lambda n, i: (l, 0, 0)),
                  halo(1), halo(2)],
        out_specs=[pl.BlockSpec((tm, d), lambda n, i: (n * nt + i, 0)),
                   pl.BlockSpec((8, d), lambda n, i: (n * nt + i, 0))],
        out_shape=[jax.ShapeDtypeStruct((rows, d), BF16),
                   jax.ShapeDtypeStruct((nseq * nt * 8, d), F32)],
        compiler_params=_params("parallel", "parallel"),
        name="merge_seq",
    )(p, p, p, p, p, p, o, cw, p, p)


def _merge_step(p, o, cw, l, hist1, hist2, steps, d):
    rows = p.shape[0]
    tm = _pick_tile(rows, 256)
    assert tm % steps == 0

    def blk(c):
        return pl.BlockSpec((tm, d), lambda i: (i, c))

    full = pl.BlockSpec((tm, d), lambda i: (i, 0))
    return pl.pallas_call(
        functools.partial(_merge_kernel, period=steps),
        grid=(rows // tm,),
        in_specs=[blk(0), blk(1), blk(2), blk(5), blk(6), blk(7), full,
                  pl.BlockSpec((None, CONV_W, d), lambda i: (l, 0, 0)), full, full],
        out_specs=[full, full],
        out_shape=[jax.ShapeDtypeStruct((rows, d), BF16),
                   jax.ShapeDtypeStruct((rows, d), F32)],
        compiler_params=_params("parallel"),
        name="merge_step",
    )(p, p, p, p, p, p, o, cw, hist1, hist2)


def _trunk(x, weights, depth, d, a_lo, rank, *, nseq, state_conv=None, state_gla=None):
    (ln_in_w, ln_in_b, w_in_t, wa2, conv_w, b_a2, gn_w, w_o_b, ln1_w, ln1_b,
     w1, w2_b, ln2_w, ln2_b) = weights
    alpha = (2 * depth) ** 0.25
    rows = x.shape[0]
    steps = rows // nseq
    h, hb = _ln(x, ln_in_w, ln_in_b)
    conv_new, gs = [], None
    for l in range(depth):
        gk = _gate(hb, w_in_t, wa2, b_a2, l, a_lo)
        p = _mmt(hb, w_in_t, l, 8 * d, a_lo, rank)
        if state_gla is None:
            o, gs = _gla_seq(p, gk, gn_w, l, gs, depth, nseq, d)
            m, utail = _merge_seq(p, o, conv_w, l, nseq, d)
            nt = utail.shape[0] // (8 * nseq)
            cs = utail.reshape(nseq, nt, 8, d)[:, nt - 1, 8 - (CONV_W - 1):, :]
        else:
            o, gs = _gla_step(p, gk, gn_w, state_gla, l, gs, steps, d)
            st = state_conv[l]
            zero = jnp.zeros((nseq, steps - 1, d), F32)
            hist1 = jnp.concatenate([st[:, 1:2], zero], axis=1).reshape(rows, d)
            hist2 = jnp.concatenate([st, zero[:, :steps - 2]], axis=1).reshape(rows, d)
            m, u = _merge_step(p, o, conv_w, l, hist1, hist2, steps, d)
            cs = u.reshape(nseq, steps, d)[:, steps - (CONV_W - 1):, :]
        h, hb = _mm_res_ln(m, w_o_b, l, h, ln1_w, ln1_b, alpha)
        f = _mm(hb, w1, l, relu2=True)
        h, hb = _mm_res_ln(f, w2_b, l, h, ln2_w, ln2_b, alpha)
        conv_new.append(cs)
    return h, jnp.stack(conv_new), gs


def kernel(x_prompt, x_sample, state_conv, state_gla, meta, ln_in_w, ln_in_b, w_in, conv_w, w_a2,
           b_a2, gn_w, w_o, ln1_w, ln1_b, w1, w2, ln2_w, ln2_b):
    depth, d, _ = w_in.shape
    nb, seq, _ = x_prompt.shape
    ns, steps, _ = x_sample.shape
    n_meta = meta.shape[0]
    rank = w_a2.shape[1]
    a_lo = 6 * d
    assert w_in.shape[2] == 8 * d + rank and steps >= CONV_W - 1 and rank < LANES

    def per_layer(v):
        return v.reshape(depth, 1, v.shape[-1])

    w_in_t = jnp.swapaxes(w_in, 1, 2)
    wa2 = jnp.pad(w_a2, ((0, 0), (0, LANES - rank), (0, 0))).astype(BF16)
    weights = (ln_in_w, ln_in_b, w_in_t, wa2, conv_w, b_a2, per_layer(gn_w), w_o.astype(BF16),
               per_layer(ln1_w), per_layer(ln1_b), w1, w2.astype(BF16),
               per_layer(ln2_w), per_layer(ln2_b))

    hp = jnp.concatenate([jnp.broadcast_to(meta.astype(F32), (nb, n_meta, d)), x_prompt], axis=1)
    length = n_meta + seq
    yp, conv_p, gla_p = _trunk(hp.reshape(nb * length, d), weights, depth, d, a_lo, rank, nseq=nb)
    y_prompt = yp.reshape(nb, length, d)[:, n_meta:]

    ys, conv_s, gla_s = _trunk(x_sample.reshape(ns * steps, d), weights, depth, d, a_lo, rank,
                               nseq=ns, state_conv=state_conv, state_gla=state_gla)
    y_sample = ys.reshape(ns, steps, d)
    return (y_prompt, y_sample, conv_p, gla_p, conv_s, gla_s)
```

```python
import functools

import jax
import jax.numpy as jnp
from jax import lax
from jax.experimental import pallas as pl
from jax.experimental.pallas import tpu as pltpu

GLA_HEADS = 4
HEADS_PER_STEP = 2
GATE_NORM = 16.0
CONV_W = 3
CHUNK = 64
SUB = 8
EPS = 1e-5
LOG2E = 1.4426950408889634
LANES = 128
VMEM_LIMIT = 56 * 1024 * 1024

F32 = jnp.float32
BF16 = jnp.bfloat16


def _params(*sem):
    return pltpu.CompilerParams(dimension_semantics=sem, vmem_limit_bytes=VMEM_LIMIT)


def _pick_tile(n, cap):
    best = None
    for t in range(8, min(n, cap) + 1, 8):
        if n % t == 0:
            best = t
    assert best is not None, (n, cap)
    return best


def _layer_norm(x, w, b):
    mu = jnp.mean(x, axis=-1, keepdims=True)
    xc = x - mu
    var = jnp.mean(xc * xc, axis=-1, keepdims=True)
    return xc * lax.rsqrt(var + EPS) * w + b


def _split3(x):
    hi = x.astype(BF16)
    r1 = x - hi.astype(F32)
    mid = r1.astype(BF16)
    lo = (r1 - mid.astype(F32)).astype(BF16)
    return hi, mid, lo


def _ln_kernel(x_ref, w_ref, b_ref, o_ref, ob_ref):
    y = _layer_norm(x_ref[...], w_ref[...], b_ref[...])
    o_ref[...] = y
    ob_ref[...] = y.astype(BF16)


def _ln(x, w, b):
    r, d = x.shape
    tm = _pick_tile(r, 1024)
    return pl.pallas_call(
        _ln_kernel,
        grid=(r // tm,),
        in_specs=[pl.BlockSpec((tm, d), lambda i: (i, 0)),
                  pl.BlockSpec((1, d), lambda i: (0, 0)),
                  pl.BlockSpec((1, d), lambda i: (0, 0))],
        out_specs=[pl.BlockSpec((tm, d), lambda i: (i, 0)),
                   pl.BlockSpec((tm, d), lambda i: (i, 0))],
        out_shape=[jax.ShapeDtypeStruct((r, d), F32), jax.ShapeDtypeStruct((r, d), BF16)],
        compiler_params=_params("parallel"),
        name="ln_in",
    )(x, w.reshape(1, d), b.reshape(1, d))


def _gate_kernel(hb_ref, wa1_ref, wa2_ref, ba2_ref, gk_ref):
    a1 = lax.dot_general(hb_ref[...], wa1_ref[...].astype(BF16), (((1,), (1,)), ((), ())),
                         preferred_element_type=F32)
    z = jnp.dot(a1.astype(BF16), wa2_ref[...], preferred_element_type=F32) + ba2_ref[...]
    ls = jnp.minimum(z, 0.0) - jnp.log1p(jnp.exp(-jnp.abs(z)))
    gk_ref[...] = ls * (LOG2E / GATE_NORM)


def _gate(hb, w_in_t, wa2, b_a2, l, a_lo):
    r, d = hb.shape
    hk = wa2.shape[2]
    tm = _pick_tile(r, 1024)
    return pl.pallas_call(
        _gate_kernel,
        grid=(r // tm,),
        in_specs=[pl.BlockSpec((tm, d), lambda i: (i, 0)),
                  pl.BlockSpec((None, LANES, d), lambda i: (l, a_lo // LANES, 0)),
                  pl.BlockSpec((None, LANES, hk), lambda i: (l, 0, 0)),
                  pl.BlockSpec((None, 1, hk), lambda i: (l, 0, 0))],
        out_specs=pl.BlockSpec((tm, hk), lambda i: (i, 0)),
        out_shape=jax.ShapeDtypeStruct((r, hk), F32),
        compiler_params=_params("parallel"),
        name="gate",
    )(hb, w_in_t, wa2, b_a2.reshape(b_a2.shape[0], 1, hk))


W_ROWS = 512


def _mm_kernel(x_ref, w_ref, o_ref, wbf_ref, *, relu2):
    @pl.when(pl.program_id(1) == 0)
    def _():
        for r0 in range(0, w_ref.shape[0], W_ROWS):
            wbf_ref[r0:r0 + W_ROWS, :] = w_ref[r0:r0 + W_ROWS, :].astype(BF16)

    acc = jnp.dot(x_ref[...], wbf_ref[...], preferred_element_type=F32)
    if relu2:
        acc = jnp.square(jnp.maximum(acc, 0.0))
    o_ref[...] = acc.astype(o_ref.dtype)


def _mm(x, w, l, relu2):
    r, k = x.shape
    n = w.shape[2]
    tm = _pick_tile(r, 2064)
    tn = 1024
    assert n % tn == 0 and k % W_ROWS == 0
    return pl.pallas_call(
        functools.partial(_mm_kernel, relu2=relu2),
        grid=(n // tn, r // tm),
        in_specs=[pl.BlockSpec((tm, k), lambda j, i: (i, 0)),
                  pl.BlockSpec((None, k, tn), lambda j, i: (l, 0, j))],
        out_specs=pl.BlockSpec((tm, tn), lambda j, i: (i, j)),
        out_shape=jax.ShapeDtypeStruct((r, n), BF16),
        scratch_shapes=[pltpu.VMEM((k, tn), BF16)],
        compiler_params=_params("arbitrary", "arbitrary"),
        name="mm_relu2" if relu2 else "mm",
    )(x, w)


def _mmt_kernel(x_ref, wa_ref, wb_ref, o_ref, wbf_ref, *, n_aligned, skip):
    j = pl.program_id(0)
    tn = wa_ref.shape[0]

    @pl.when(pl.program_id(1) == 0)
    def _():
        @pl.when(j < n_aligned)
        def _():
            for r0 in range(0, tn, W_ROWS):
                wbf_ref[r0:r0 + W_ROWS, :] = wa_ref[r0:r0 + W_ROWS, :].astype(BF16)

        @pl.when(j >= n_aligned)
        def _():
            for r0 in range(0, tn, W_ROWS):
                hi = min(r0 + W_ROWS, tn - skip)
                wbf_ref[r0:hi, :] = wa_ref[r0 + skip:hi + skip, :].astype(BF16)
            wbf_ref[tn - skip:, :] = wb_ref[...].astype(BF16)

    acc = lax.dot_general(x_ref[...], wbf_ref[...], (((1,), (1,)), ((), ())),
                          preferred_element_type=F32)
    o_ref[...] = acc.astype(o_ref.dtype)


def _mmt(x, w_t, l, n, skip_lo, skip):
    r, k = x.shape
    tm = _pick_tile(r, 2064)
    tn = 1024
    assert n % tn == 0 and skip_lo % tn == 0 and skip % 16 == 0 and tn % skip == 0
    assert w_t.shape[1] == n + skip and tn % W_ROWS == 0
    n_aligned = skip_lo // tn
    per = tn // skip
    return pl.pallas_call(
        functools.partial(_mmt_kernel, n_aligned=n_aligned, skip=skip),
        grid=(n // tn, r // tm),
        in_specs=[pl.BlockSpec((tm, k), lambda j, i: (i, 0)),
                  pl.BlockSpec((None, tn, k), lambda j, i: (l, j, 0)),
                  pl.BlockSpec((None, skip, k),
                               lambda j, i: (l, jnp.maximum(j, n_aligned) * per + per, 0))],
        out_specs=pl.BlockSpec((tm, tn), lambda j, i: (i, j)),
        out_shape=jax.ShapeDtypeStruct((r, n), BF16),
        scratch_shapes=[pltpu.VMEM((tn, k), BF16)],
        compiler_params=_params("arbitrary", "arbitrary"),
        name="mm_proj",
    )(x, w_t, w_t)


def _mm_res_ln_kernel(a_ref, w_ref, res_ref, lw_ref, lb_ref, o_ref, ob_ref, *, alpha, nk):
    kk = pl.program_id(1)
    part = jnp.dot(a_ref[...], w_ref[...], preferred_element_type=F32)

    def finish(acc):
        y = _layer_norm(alpha * res_ref[...] + acc, lw_ref[...], lb_ref[...])
        o_ref[...] = y
        ob_ref[...] = y.astype(BF16)

    if nk == 1:
        finish(part)
        return

    @pl.when(kk == 0)
    def _():
        o_ref[...] = part

    @pl.when((kk > 0) & (kk < nk - 1))
    def _():
        o_ref[...] += part

    @pl.when(kk == nk - 1)
    def _():
        finish(o_ref[...] + part)


def _mm_res_ln(a, w, l, res, lw, lb, alpha):
    r, k = a.shape
    d = w.shape[2]
    tk = 1024 if k > 2048 and k % 1024 == 0 else k
    nk = k // tk
    tm = _pick_tile(r, 688 if nk > 1 else 344)
    return pl.pallas_call(
        functools.partial(_mm_res_ln_kernel, alpha=alpha, nk=nk),
        grid=(r // tm, nk),
        in_specs=[pl.BlockSpec((tm, tk), lambda i, kk: (i, kk)),
                  pl.BlockSpec((None, tk, d), lambda i, kk: (l, kk, 0)),
                  pl.BlockSpec((tm, d), lambda i, kk: (i, 0)),
                  pl.BlockSpec((None, 1, d), lambda i, kk: (l, 0, 0)),
                  pl.BlockSpec((None, 1, d), lambda i, kk: (l, 0, 0))],
        out_specs=[pl.BlockSpec((tm, d), lambda i, kk: (i, 0)),
                   pl.BlockSpec((tm, d), lambda i, kk: (i, 0))],
        out_shape=[jax.ShapeDtypeStruct((r, d), F32), jax.ShapeDtypeStruct((r, d), BF16)],
        compiler_params=_params("parallel", "arbitrary"),
        name="mm_res_ln",
    )(a, w, res, lw, lb)


def _cum_matrices(c, group):
    t = lax.broadcasted_iota(jnp.int32, (c, c), 0)
    u = lax.broadcasted_iota(jnp.int32, (c, c), 1)
    same = (t // group) == (u // group)
    tri = same & (u <= t)
    if group == c:
        tend = (u // SUB) <= (t // SUB)
    else:
        tend = same
    m = jnp.concatenate([tri, tend], axis=0).astype(BF16)
    return jnp.concatenate([m, m, m], axis=1)


def _cumsum2(cm, gk):
    c = gk.shape[0]
    hi, mid, lo = _split3(gk)
    g3 = jnp.concatenate([hi, mid, lo], axis=0)
    be = jnp.dot(cm, g3, preferred_element_type=F32)
    return be[:c], be[c:]


def _tiles(x):
    return x.reshape(x.shape[0] // 8, 8, x.shape[1])


def _diag_lags(q3, k3, b3, nlag):
    cols = []
    for d in range(nlag):
        if d == 0:
            p = q3 * k3
        else:
            kr = pltpu.roll(k3, d, axis=1)
            br = pltpu.roll(b3, d, axis=1)
            p = q3 * kr * jnp.exp2(jnp.minimum(b3 - br, 0.0))
        cols.append(jnp.sum(p, axis=-1, keepdims=True))
    return cols


def _decay_columns(b_last_row):
    dk = b_last_row.shape[1]
    hi, mid, lo = _split3(b_last_row)
    lm = jnp.concatenate([hi, mid, lo, jnp.zeros((16 - 3, dk), BF16)], axis=0)
    ones = jnp.ones((16, LANES), BF16)
    col = lax.dot_general(lm, ones, (((0,), (0,)), ((), ())), preferred_element_type=F32)
    return jnp.exp2(col)


def _gla_chunk(q_ref, k_ref, v_ref, gk_ref, gnw, o_ref, s_ref, r0, c, cm, scale, hh):
    dk, dv = s_ref.shape[1], s_ref.shape[2]
    rows = pl.ds(r0, c)
    kcols = slice(hh * dk, (hh + 1) * dk)
    vcols = slice(hh * dv, (hh + 1) * dv)
    gk = gk_ref[rows, kcols]
    qf = q_ref[rows, kcols].astype(F32) * scale
    kf = k_ref[rows, kcols].astype(F32)
    vb = v_ref[rows, vcols]
    nsub = c // SUB

    b, eb = _cumsum2(cm, gk)
    b_last = b[c - 1:c, :]

    s_bf = s_ref[hh].astype(BF16)
    o = jnp.dot((qf * jnp.exp2(b)).astype(BF16), s_bf, preferred_element_type=F32)

    g = c // 8
    sub = lax.broadcasted_iota(jnp.int32, (g, 8, c), 1)
    row = 8 * lax.broadcasted_iota(jnp.int32, (g, 8, c), 0) + sub
    lane = lax.broadcasted_iota(jnp.int32, (g, 8, c), 2)
    a3 = jnp.zeros((g, 8, c), F32)
    for d, col in enumerate(_diag_lags(_tiles(qf), _tiles(kf), _tiles(b), SUB)):
        a3 = jnp.where((lane == row - d) & (sub >= d), col, a3)
    a = a3.reshape(c, c)

    khat = kf * jnp.exp2(eb - b)
    if nsub > 1:
        pieces = []
        for j in range(nsub - 1):
            lo = SUB * (j + 1)
            pieces.append(qf[lo:] * jnp.exp2(b[lo:] - eb[SUB * j:SUB * j + 1, :]))
        qs = jnp.concatenate(pieces, axis=0).astype(BF16)
        prod = lax.dot_general(qs, khat.astype(BF16), (((1,), (1,)), ((), ())),
                               preferred_element_type=F32)
        off = 0
        for j in range(nsub - 1):
            lo = SUB * (j + 1)
            seg = prod[off:off + c - lo, :]
            off += c - lo
            lane_j = lax.broadcasted_iota(jnp.int32, seg.shape, 1)
            seg = jnp.where((lane_j >= SUB * j) & (lane_j < SUB * (j + 1)), seg, 0.0)
            a = a + jnp.concatenate([jnp.zeros((lo, c), F32), seg], axis=0)

    o = o + jnp.dot(a.astype(BF16), vb, preferred_element_type=F32)
    o = o * lax.rsqrt(jnp.mean(o * o, axis=-1, keepdims=True) + EPS) * gnw
    o_ref[rows, vcols] = o.astype(o_ref.dtype)

    kl = (khat * jnp.exp2(b_last - eb)).astype(BF16)
    upd = lax.dot_general(kl, vb, (((0,), (0,)), ((), ())), preferred_element_type=F32)
    edec = _decay_columns(b_last)
    for cb in range(dv // LANES):
        sl = slice(cb * LANES, (cb + 1) * LANES)
        s_ref[hh, :, sl] = s_ref[hh, :, sl] * edec + upd[:, sl]


def _gla_seq_kernel(q_ref, k_ref, v_ref, gk_ref, gnw_ref, o_ref, s_ref, *, lead, scale):
    length = q_ref.shape[0]
    gnw = gnw_ref[...]
    s_ref[...] = jnp.zeros_like(s_ref)
    if lead:
        cm_lead = _cum_matrices(lead, lead)
        for hh in range(HEADS_PER_STEP):
            _gla_chunk(q_ref, k_ref, v_ref, gk_ref, gnw, o_ref, s_ref, 0, lead, cm_lead, scale, hh)
    n_chunks = (length - lead) // CHUNK
    cm = _cum_matrices(CHUNK, CHUNK)

    def body(ci, carry):
        r0 = pl.multiple_of(lead + ci * CHUNK, SUB)
        for hh in range(HEADS_PER_STEP):
            _gla_chunk(q_ref, k_ref, v_ref, gk_ref, gnw, o_ref, s_ref, r0, CHUNK, cm, scale, hh)
        return carry

    lax.fori_loop(0, n_chunks, body, 0)


def _drop_ref(fn, idx):
    def wrapped(*refs):
        return fn(*refs[:idx], *refs[idx + 1:])
    return wrapped


def _gla_seq(p, gk, gnw, l, prev_state, depth, nseq, d):
    rows = p.shape[0]
    length = rows // nseq
    dk, dv = d // 2 // GLA_HEADS, d // GLA_HEADS
    hp = HEADS_PER_STEP
    lead = (length % CHUNK)
    assert lead % SUB == 0 and length % 8 == 0 and GLA_HEADS % hp == 0
    qb, kb, vb = 3 * d // (hp * dk), (3 * d + GLA_HEADS * dk) // (hp * dk), 4 * d // (hp * dv)
    body = functools.partial(_gla_seq_kernel, lead=lead, scale=float(dk) ** -0.5)
    in_specs = [pl.BlockSpec((length, hp * dk), lambda n, h: (n, qb + h)),
                pl.BlockSpec((length, hp * dk), lambda n, h: (n, kb + h)),
                pl.BlockSpec((length, hp * dv), lambda n, h: (n, vb + h)),
                pl.BlockSpec((length, hp * dk), lambda n, h: (n, h)),
                pl.BlockSpec((None, 1, dv), lambda n, h: (l, 0, 0))]
    args = [p, p, p, gk, gnw]
    aliases = {}
    if prev_state is not None:
        in_specs.append(pl.BlockSpec(memory_space=pl.ANY))
        args.append(prev_state)
        aliases = {5: 1}
        body = _drop_ref(body, 5)
    return pl.pallas_call(
        body,
        grid=(nseq, GLA_HEADS // hp),
        in_specs=in_specs,
        out_specs=[pl.BlockSpec((length, hp * dv), lambda n, h: (n, h)),
                   pl.BlockSpec((None, None, hp, dk, dv), lambda n, h: (l, n, h, 0, 0))],
        out_shape=[jax.ShapeDtypeStruct((rows, d), BF16),
                   jax.ShapeDtypeStruct((depth, nseq, GLA_HEADS, dk, dv), F32)],
        input_output_aliases=aliases,
        compiler_params=_params("parallel", "parallel"),
        name="gla_seq",
    )(*args)


def _gla_step_kernel(q_ref, k_ref, v_ref, gk_ref, gnw_ref, s_in_ref, o_ref, s_ref, *, steps, scale):
    c = q_ref.shape[0]
    per8 = 8 // steps
    gk = gk_ref[...]
    qf = q_ref[...].astype(F32) * scale
    kf = k_ref[...].astype(F32)
    vb = v_ref[...]
    v3 = _tiles(vb.astype(F32))

    b, eb = _cumsum2(_cum_matrices(c, steps), gk)
    sub = lax.broadcasted_iota(jnp.int32, (c // 8, 8, 1), 1)
    o3 = jnp.zeros(v3.shape, F32)
    for d, col in enumerate(_diag_lags(_tiles(qf), _tiles(kf), _tiles(b), steps)):
        vr = v3 if d == 0 else pltpu.roll(v3, d, axis=1)
        o3 = o3 + jnp.where((sub % steps) >= d, col, 0.0) * vr
    o = o3.reshape(c, v3.shape[2])

    qe = (qf * jnp.exp2(b)).astype(BF16)
    kl = kf * jnp.exp2(eb - b)
    row8 = lax.broadcasted_iota(jnp.int32, (8, 1), 0)
    for t8 in range(c // 8):
        r8 = slice(8 * t8, 8 * t8 + 8)
        acc = jnp.zeros((8, o.shape[1]), F32)
        for w in range(per8):
            n = t8 * per8 + w
            mine = (row8 // steps) == w
            s_old = s_in_ref[n]
            part = jnp.dot(qe[r8], s_old.astype(BF16), preferred_element_type=F32)
            acc = jnp.where(mine, part, acc)
            kn = jnp.where(mine, kl[r8], 0.0).astype(BF16)
            upd = lax.dot_general(kn, vb[r8], (((0,), (0,)), ((), ())), preferred_element_type=F32)
            r_last = 8 * t8 + steps * (w + 1) - 1
            edec = _decay_columns(b[r_last:r_last + 1, :])
            dv = upd.shape[1]
            for cb in range(dv // LANES):
                sl = slice(cb * LANES, (cb + 1) * LANES)
                s_ref[n, :, sl] = s_old[:, sl] * edec + upd[:, sl]
        o_t = o[r8] + acc
        o_t = o_t * lax.rsqrt(jnp.mean(o_t * o_t, axis=-1, keepdims=True) + EPS) * gnw_ref[...]
        o_ref[r8, :] = o_t.astype(o_ref.dtype)


def _gla_step(p, gk, gnw, state, l, prev_state, steps, d):
    rows = p.shape[0]
    nseq = rows // steps
    dk, dv = d // 2 // GLA_HEADS, d // GLA_HEADS
    assert 8 % steps == 0
    nb = 8
    assert nseq % nb == 0
    c = nb * steps
    qb, kb, vb = 3 * d // dk, 3 * d // dk + GLA_HEADS, 4 * d // dv
    body = functools.partial(_gla_step_kernel, steps=steps, scale=float(dk) ** -0.5)
    state_spec = pl.BlockSpec((None, nb, None, dk, dv), lambda i, h: (l, i, h, 0, 0))
    in_specs = [pl.BlockSpec((c, dk), lambda i, h: (i, qb + h)),
                pl.BlockSpec((c, dk), lambda i, h: (i, kb + h)),
                pl.BlockSpec((c, dv), lambda i, h: (i, vb + h)),
                pl.BlockSpec((c, dk), lambda i, h: (i, h)),
                pl.BlockSpec((None, 1, dv), lambda i, h: (l, 0, 0)),
                state_spec]
    args = [p, p, p, gk, gnw, state]
    aliases = {}
    if prev_state is not None:
        in_specs.append(pl.BlockSpec(memory_space=pl.ANY))
        args.append(prev_state)
        aliases = {6: 1}
        body = _drop_ref(body, 6)
    return pl.pallas_call(
        body,
        grid=(nseq // nb, GLA_HEADS),
        in_specs=in_specs,
        out_specs=[pl.BlockSpec((c, dv), lambda i, h: (i, h)), state_spec],
        out_shape=[jax.ShapeDtypeStruct((rows, d), BF16),
                   jax.ShapeDtypeStruct(state.shape, F32)],
        input_output_aliases=aliases,
        compiler_params=_params("parallel", "parallel"),
        name="gla_step",
    )(*args)


def _merge_kernel(cb_ref, cc_ref, cx_ref, g_ref, ga_ref, gb_ref, o_ref, cw_ref,
                  h1_ref, h2_ref, m_ref, u_ref, *, period):
    tm = cc_ref.shape[0]
    u = cc_ref[...].astype(F32) * cx_ref[...].astype(F32)
    row = lax.broadcasted_iota(jnp.int32, (tm, 1), 0)
    u1 = pltpu.roll(u, 1, axis=0)
    u2 = pltpu.roll(u, 2, axis=0)
    if period:
        t = row % period
        u1 = jnp.where(t >= 1, u1, h1_ref[...])
        u2 = jnp.where(t >= 2, u2, h2_ref[...])
    else:
        first = pl.program_id(1) == 0
        halo = h1_ref[...].astype(F32) * h2_ref[...].astype(F32)
        halo = jnp.where(first, 0.0, halo)
        u1 = jnp.where(row == 0, halo[7:8, :], u1)
        u2 = jnp.where(row == 0, halo[6:7, :], jnp.where(row == 1, halo[7:8, :], u2))
    cw = cw_ref[...]
    conv = u2 * cw[0:1, :] + u1 * cw[1:2, :] + u * cw[2:3, :]
    y_a = cb_ref[...].astype(F32) * conv
    y_b = o_ref[...].astype(F32) * jax.nn.silu(g_ref[...].astype(F32))
    m = jax.nn.sigmoid(ga_ref[...].astype(F32)) * y_a + jax.nn.sigmoid(gb_ref[...].astype(F32)) * y_b
    m_ref[...] = m.astype(BF16)
    if period:
        u_ref[...] = u
    else:
        u_ref[...] = u[tm - 8:, :]


def _merge_seq(p, o, cw, l, nseq, d):
    rows = p.shape[0]
    length = rows // nseq
    tm = _pick_tile(length, 344)
    nt = length // tm

    def blk(c):
        return pl.BlockSpec((tm, d), lambda n, i: (n * nt + i, c))

    def halo(c):
        return pl.BlockSpec((8, d), lambda n, i: (jnp.maximum((n * nt + i) * (tm // 8) - 1, 0), c))

    return pl.pallas_call(
        functools.partial(_merge_kernel, period=0),
        grid=(nseq, nt),
        in_specs=[blk(0), blk(1), blk(2), blk(5), blk(6), blk(7),
                  pl.BlockSpec((tm, d), lambda n, i: (n * nt + i, 0)),
                  pl.BlockSpec((None, CONV_W, d), lambda n, i: (l, 0, 0)),
                  halo(1), halo(2)],
        out_specs=[pl.BlockSpec((tm, d), lambda n, i: (n * nt + i, 0)),
                   pl.BlockSpec((8, d), lambda n, i: (n * nt + i, 0))],
        out_shape=[jax.ShapeDtypeStruct((rows, d), BF16),
                   jax.ShapeDtypeStruct((nseq * nt * 8, d), F32)],
        compiler_params=_params("parallel", "parallel"),
        name="merge_seq",
    )(p, p, p, p, p, p, o, cw, p, p)


def _merge_step(p, o, cw, l, hist1, hist2, steps, d):
    rows = p.shape[0]
    tm = _pick_tile(rows, 256)
    assert tm % steps == 0

    def blk(c):
        return pl.BlockSpec((tm, d), lambda i: (i, c))

    full = pl.BlockSpec((tm, d), lambda i: (i, 0))
    return pl.pallas_call(
        functools.partial(_merge_kernel, period=steps),
        grid=(rows // tm,),
        in_specs=[blk(0), blk(1), blk(2), blk(5), blk(6), blk(7), full,
                  pl.BlockSpec((None, CONV_W, d), lambda i: (l, 0, 0)), full, full],
        out_specs=[full, full],
        out_shape=[jax.ShapeDtypeStruct((rows, d), BF16),
                   jax.ShapeDtypeStruct((rows, d), F32)],
        compiler_params=_params("parallel"),
        name="merge_step",
    )(p, p, p, p, p, p, o, cw, hist1, hist2)


def _trunk(x, weights, depth, d, a_lo, rank, *, nseq, state_conv=None, state_gla=None):
    (ln_in_w, ln_in_b, w_in_t, wa2, conv_w, b_a2, gn_w, w_o_b, ln1_w, ln1_b,
     w1, w2_b, ln2_w, ln2_b) = weights
    alpha = (2 * depth) ** 0.25
    rows = x.shape[0]
    steps = rows // nseq
    h, hb = _ln(x, ln_in_w, ln_in_b)
    conv_new, gs = [], None
    for l in range(depth):
        gk = _gate(hb, w_in_t, wa2, b_a2, l, a_lo)
        p = _mmt(hb, w_in_t, l, 8 * d, a_lo, rank)
        if state_gla is None:
            o, gs = _gla_seq(p, gk, gn_w, l, gs, depth, nseq, d)
            m, utail = _merge_seq(p, o, conv_w, l, nseq, d)
            nt = utail.shape[0] // (8 * nseq)
            cs = utail.reshape(nseq, nt, 8, d)[:, nt - 1, 8 - (CONV_W - 1):, :]
        else:
            o, gs = _gla_step(p, gk, gn_w, state_gla, l, gs, steps, d)
            st = state_conv[l]
            zero = jnp.zeros((nseq, steps - 1, d), F32)
            hist1 = jnp.concatenate([st[:, 1:2], zero], axis=1).reshape(rows, d)
            hist2 = jnp.concatenate([st, zero[:, :steps - 2]], axis=1).reshape(rows, d)
            m, u = _merge_step(p, o, conv_w, l, hist1, hist2, steps, d)
            cs = u.reshape(nseq, steps, d)[:, steps - (CONV_W - 1):, :]
        h, hb = _mm_res_ln(m, w_o_b, l, h, ln1_w, ln1_b, alpha)
        f = _mm(hb, w1, l, relu2=True)
        h, hb = _mm_res_ln(f, w2_b, l, h, ln2_w, ln2_b, alpha)
        conv_new.append(cs)
    return h, jnp.stack(conv_new), gs


def kernel(x_prompt, x_sample, state_conv, state_gla, meta, ln_in_w, ln_in_b, w_in, conv_w, w_a2,
           b_a2, gn_w, w_o, ln1_w, ln1_b, w1, w2, ln2_w, ln2_b):
    depth, d, _ = w_in.shape
    nb, seq, _ = x_prompt.shape
    ns, steps, _ = x_sample.shape
    n_meta = meta.shape[0]
    rank = w_a2.shape[1]
    a_lo = 6 * d
    assert w_in.shape[2] == 8 * d + rank and steps >= CONV_W - 1 and rank < LANES

    def per_layer(v):
        return v.reshape(depth, 1, v.shape[-1])

    w_in_t = jnp.swapaxes(w_in, 1, 2)
    wa2 = jnp.pad(w_a2, ((0, 0), (0, LANES - rank), (0, 0))).astype(BF16)
    weights = (ln_in_w, ln_in_b, w_in_t, wa2, conv_w, b_a2, per_layer(gn_w), w_o.astype(BF16),
               per_layer(ln1_w), per_layer(ln1_b), w1, w2.astype(BF16),
               per_layer(ln2_w), per_layer(ln2_b))

    hp = jnp.concatenate([jnp.broadcast_to(meta.astype(F32), (nb, n_meta, d)), x_prompt], axis=1)
    length = n_meta + seq
    yp, conv_p, gla_p = _trunk(hp.reshape(nb * length, d), weights, depth, d, a_lo, rank, nseq=nb)
    y_prompt = yp.reshape(nb, length, d)[:, n_meta:]

    ys, conv_s, gla_s = _trunk(x_sample.reshape(ns * steps, d), weights, depth, d, a_lo, rank,
                               nseq=ns, state_conv=state_conv, state_gla=state_gla)
    y_sample = ys.reshape(ns, steps, d)
    return (y_prompt, y_sample, conv_p, gla_p, conv_s, gla_s)
```
